```python
import jax, jax.numpy as jnp
from jax import lax
import numpy as np

D_MODEL = 4096
BATCH = 2
SEQ = 8192
DEPTH = 4

N_MIXERS = 3
EPS = 1e-6
HEAD_DIM = 128
D_FF = (11 * D_MODEL) // 8
GDN_HEADS = D_MODEL // HEAD_DIM
GDN_DK = HEAD_DIM
GDN_DV = HEAD_DIM
GDN_CONV = 4
GDN_CHUNK = 64
SC_CONV = 3
ATT_HEADS = D_MODEL // HEAD_DIM
ATT_DH = HEAD_DIM
MOBA_BLOCK = 256
MOBA_TOPK = 3
MOBA_QCHUNK = 16
ROPE_THETA = 10000.0

N_A = (DEPTH + 2) // 3
N_B = (DEPTH + 1) // 3
N_C = DEPTH // 3
GDN_QK = GDN_HEADS * GDN_DK
GDN_VW = GDN_HEADS * GDN_DV
GDN_CONV_W = 2 * GDN_QK + GDN_VW
GDN_PROJ = 2 * GDN_QK + 2 * GDN_VW + 2 * GDN_HEADS

kernel_name = "macaron_hybrid_gdn_shortconv_moba"


def rmsnorm(x, g):
    xf = x.astype(jnp.float32)
    y = xf * lax.rsqrt(jnp.mean(xf * xf, axis=-1, keepdims=True) + EPS)
    return (y * g.astype(jnp.float32)).astype(x.dtype)


def l2norm(x):
    return x * lax.rsqrt(jnp.sum(x * x, axis=-1, keepdims=True) + EPS)


def swiglu(h, w_in, w_out):
    gate, up = jnp.split(h @ w_in, 2, axis=-1)
    return (jax.nn.silu(gate) * up) @ w_out


def causal_dwconv(x, w):
    width = w.shape[0]
    s = x.shape[1]
    xp = jnp.pad(x, ((0, 0), (width - 1, 0), (0, 0)))
    return sum(xp[:, t:t + s] * w[t] for t in range(width))


def rope(x, positions):
    half = x.shape[-1] // 2
    inv_freq = ROPE_THETA ** (-jnp.arange(half, dtype=jnp.float32) / half)
    ang = positions.astype(jnp.float32)[..., None] * inv_freq
    cos = jnp.cos(ang)[:, :, None, :]
    sin = jnp.sin(ang)[:, :, None, :]
    xf = x.astype(jnp.float32)
    x1, x2 = xf[..., :half], xf[..., half:]
    return jnp.concatenate([x1 * cos - x2 * sin, x2 * cos + x1 * sin], axis=-1).astype(x.dtype)


def chunk_gated_delta_rule(q, k, v, g, beta):
    b, s, h, dk = q.shape
    dv = v.shape[-1]
    c = GDN_CHUNK
    n = s // c

    def to_chunks(t):
        return jnp.moveaxis(t.reshape((b, n, c) + t.shape[2:]), 3, 1)

    q, k, v, g, beta = map(to_chunks, (q, k, v, g, beta))
    gc = jnp.cumsum(g, axis=-1)
    incl = jnp.tril(jnp.ones((c, c), dtype=bool))
    strict = jnp.tril(jnp.ones((c, c), dtype=bool), -1)
    eye = jnp.eye(c, dtype=jnp.float32)
    diff = gc[..., :, None] - gc[..., None, :]
    gamma = jnp.where(incl, jnp.exp(jnp.where(incl, diff, 0.0)), 0.0)
    kb = k * beta[..., None]
    m = jnp.where(strict, jnp.einsum('bhncd,bhnjd->bhncj', kb, k) * gamma, 0.0) + eye
    t_inv = lax.linalg.triangular_solve(m, jnp.broadcast_to(eye, m.shape), left_side=True,
                                        lower=True, unit_diagonal=True)
    w = t_inv @ (kb * jnp.exp(gc)[..., None])
    u = t_inv @ (v * beta[..., None])
    qk = jnp.einsum('bhncd,bhnjd->bhncj', q, k) * gamma
    q_dec = q * jnp.exp(gc)[..., None]
    k_dec = k * jnp.exp(gc[..., -1:] - gc)[..., None]
    g_last = jnp.exp(gc[..., -1])

    def step(state, xs):
        w_c, u_c, qk_c, qd_c, kd_c, gl_c = xs
        v_new = u_c - w_c @ state
        o_c = qd_c @ state + qk_c @ v_new
        state = state * gl_c[..., None, None] + jnp.swapaxes(kd_c, -1, -2) @ v_new
        return state, o_c

    xs = tuple(jnp.moveaxis(t, 2, 0) for t in (w, u, qk, q_dec, k_dec, g_last))
    _, o = lax.scan(step, jnp.zeros((b, h, dk, dv), jnp.float32), xs)
    return o.transpose(1, 0, 3, 2, 4).reshape(b, s, h, dv)


def gated_deltanet(h, w_in, conv_w, a_log, dt_bias, o_norm, w_out):
    b, s, _ = h.shape
    H = GDN_HEADS
    proj = h @ w_in
    qkv = proj[..., :GDN_CONV_W]
    z = proj[..., GDN_CONV_W:GDN_CONV_W + GDN_VW]
    a = proj[..., GDN_CONV_W + GDN_VW:GDN_CONV_W + GDN_VW + H]
    bt = proj[..., GDN_CONV_W + GDN_VW + H:]
    qkv = jax.nn.silu(causal_dwconv(qkv, conv_w)).astype(jnp.float32)
    q = l2norm(qkv[..., :GDN_QK].reshape(b, s, H, GDN_DK)) * (GDN_DK ** -0.5)
    k = l2norm(qkv[..., GDN_QK:2 * GDN_QK].reshape(b, s, H, GDN_DK))
    v = qkv[..., 2 * GDN_QK:].reshape(b, s, H, GDN_DV)
    beta = jax.nn.sigmoid(bt.astype(jnp.float32))
    g = -jnp.exp(a_log.astype(jnp.float32)) * jax.nn.softplus(a.astype(jnp.float32) + dt_bias.astype(jnp.float32))
    o = chunk_gated_delta_rule(q, k, v, g, beta)
    o = rmsnorm(o, o_norm) * jax.nn.silu(z.reshape(b, s, H, GDN_DV).astype(jnp.float32))
    return o.reshape(b, s, GDN_VW).astype(h.dtype) @ w_out


def short_conv(h, w_in, conv_w, w_out):
    gb, gc, xv = jnp.split(h @ w_in, 3, axis=-1)
    return (gb * causal_dwconv(gc * xv, conv_w)) @ w_out


def moba_attention(h, positions, w_in, q_norm, k_norm, w_out):
    b, s, _ = h.shape
    H, dh, bs = ATT_HEADS, ATT_DH, MOBA_BLOCK
    q, k, v = (t.reshape(b, s, H, dh) for t in jnp.split(h @ w_in, 3, axis=-1))
    q = rope(rmsnorm(q, q_norm), positions)
    k = rope(rmsnorm(k, k_norm), positions)
    nb = -(-s // bs)
    sp = nb * bs
    padw = ((0, 0), (0, 0), (0, sp - s), (0, 0))
    qp, kp, vp = (jnp.pad(jnp.transpose(t, (0, 2, 1, 3)), padw) for t in (q, k, v))
    kb = kp.reshape(b, H, nb, bs, dh)
    vb = vp.reshape(b, H, nb, bs, dh)
    kmean = jnp.mean(kb.astype(jnp.float32), axis=3)
    topk = min(MOBA_TOPK, nb - 1)
    qc = MOBA_QCHUNK
    scale = dh ** -0.5
    bi = jnp.arange(b)[:, None, None, None]
    hi = jnp.arange(H)[None, :, None, None]

    def chunk(ci):
        start = ci * qc
        qblk = start // bs
        q_c = lax.dynamic_slice_in_dim(qp, start, qc, axis=2)
        qpos = start + jnp.arange(qc)
        k_own = lax.dynamic_index_in_dim(kb, qblk, axis=2, keepdims=False)
        v_own = lax.dynamic_index_in_dim(vb, qblk, axis=2, keepdims=False)
        kpos = qblk * bs + jnp.arange(bs)
        s_own = jnp.einsum('bhqd,bhkd->bhqk', q_c, k_own).astype(jnp.float32) * scale
        s_own = jnp.where(kpos[None, :] <= qpos[:, None], s_own, -jnp.inf)
        if topk > 0:
            gate = jnp.einsum('bhqd,bhnd->bhqn', q_c.astype(jnp.float32), kmean)
            gate = jnp.where(jnp.arange(nb) < qblk, gate, -jnp.inf)
            _, idx = lax.top_k(gate, topk)
            valid = jnp.arange(topk) < qblk
            k_sel = kb[bi, hi, idx]
            v_sel = vb[bi, hi, idx]
            s_sel = jnp.einsum('bhqd,bhqjkd->bhqjk', q_c, k_sel).astype(jnp.float32) * scale
            s_sel = jnp.where(valid[:, None], s_sel, -jnp.inf).reshape(b, H, qc, topk * bs)
            p = jax.nn.softmax(jnp.concatenate([s_sel, s_own], axis=-1), axis=-1)
            p_sel = p[..., :topk * bs].reshape(b, H, qc, topk, bs).astype(vb.dtype)
            p_own = p[..., topk * bs:].astype(vb.dtype)
            return (jnp.einsum('bhqjk,bhqjkd->bhqd', p_sel, v_sel)
                    + jnp.einsum('bhqk,bhkd->bhqd', p_own, v_own))
        p_own = jax.nn.softmax(s_own, axis=-1).astype(vb.dtype)
        return jnp.einsum('bhqk,bhkd->bhqd', p_own, v_own)

    o = lax.map(chunk, jnp.arange(sp // qc))
    o = o.transpose(1, 0, 3, 2, 4).reshape(b, sp, H * dh)[:, :s]
    return o @ w_out


def setup_inputs(seed: int = 0) -> dict:
    key = jax.random.key(seed)
    ks = iter(jax.random.split(key, 32))

    def nrm(shape, scale):
        return jax.random.normal(next(ks), shape, jnp.float32) * scale

    def gain(shape):
        return 1.0 + nrm(shape, 0.02)

    D, F = D_MODEL, D_FF
    x = nrm((BATCH, SEQ, D), 1.0)
    positions = jnp.broadcast_to(jnp.arange(SEQ, dtype=jnp.int32), (BATCH, SEQ))
    dt = jnp.exp(jax.random.uniform(next(ks), (N_A, GDN_HEADS), jnp.float32,
                                    np.log(1e-3), np.log(1e-1)))
    return {
        "x": x,
        "positions": positions,
        "norm_ffn1": gain((DEPTH, D)),
        "norm_mix": gain((DEPTH, D)),
        "norm_ffn2": gain((DEPTH, D)),
        "ffn1_w_in": nrm((DEPTH, D, 2 * F), D ** -0.5),
        "ffn1_w_out": nrm((DEPTH, F, D), F ** -0.5),
        "ffn2_w_in": nrm((DEPTH, D, 2 * F), D ** -0.5),
        "ffn2_w_out": nrm((DEPTH, F, D), F ** -0.5),
        "gdn_w_in": nrm((N_A, D, GDN_PROJ), D ** -0.5),
        "gdn_conv_w": nrm((N_A, GDN_CONV, GDN_CONV_W), GDN_CONV ** -0.5),
        "gdn_a_log": jnp.log(jax.random.uniform(next(ks), (N_A, GDN_HEADS), jnp.float32, 1.0, 16.0)),
        "gdn_dt_bias": dt + jnp.log(-jnp.expm1(-dt)),
        "gdn_out_norm": gain((N_A, GDN_DV)),
        "gdn_w_out": nrm((N_A, GDN_VW, D), GDN_VW ** -0.5),
        "sc_w_in": nrm((N_B, D, 3 * D), D ** -0.5),
        "sc_conv_w": nrm((N_B, SC_CONV, D), SC_CONV ** -0.5),
        "sc_w_out": nrm((N_B, D, D), D ** -0.5),
        "moba_w_in": nrm((N_C, D, 3 * D), D ** -0.5),
        "moba_q_norm": gain((N_C, ATT_DH)),
        "moba_k_norm": gain((N_C, ATT_DH)),
        "moba_w_out": nrm((N_C, D, D), D ** -0.5),
    }


def reference(x, positions, norm_ffn1, norm_mix, norm_ffn2, ffn1_w_in, ffn1_w_out,
              ffn2_w_in, ffn2_w_out, gdn_w_in, gdn_conv_w, gdn_a_log, gdn_dt_bias,
              gdn_out_norm, gdn_w_out, sc_w_in, sc_conv_w, sc_w_out, moba_w_in,
              moba_q_norm, moba_k_norm, moba_w_out):
    for i in range(DEPTH):
        kind, j = i % N_MIXERS, i // N_MIXERS
        x = x + 0.5 * swiglu(rmsnorm(x, norm_ffn1[i]), ffn1_w_in[i], ffn1_w_out[i])
        h = rmsnorm(x, norm_mix[i])
        if kind == 0:
            m = gated_deltanet(h, gdn_w_in[j], gdn_conv_w[j], gdn_a_log[j], gdn_dt_bias[j],
                               gdn_out_norm[j], gdn_w_out[j])
        elif kind == 1:
            m = short_conv(h, sc_w_in[j], sc_conv_w[j], sc_w_out[j])
        else:
            m = moba_attention(h, positions, moba_w_in[j], moba_q_norm[j], moba_k_norm[j],
                               moba_w_out[j])
        x = x + m
        x = x + 0.5 * swiglu(rmsnorm(x, norm_ffn2[i]), ffn2_w_in[i], ffn2_w_out[i])
    return x
```

```python
import functools

import jax
import jax.numpy as jnp
from jax import lax
from jax.experimental import pallas as pl
from jax.experimental.pallas import tpu as pltpu

F32 = jnp.float32
BF16 = jnp.bfloat16

EPS = 1e-6
HEAD_DIM = 128
GDN_CHUNK = 64
MOBA_BLOCK = 256
MOBA_TOPK = 3
ROPE_THETA = 10000.0

LANES = 128
CONV_HALO = 8
VMEM_LIMIT = 56 * 1024 * 1024


def _tile(dim, want):
    t = min(dim, want)
    while dim % t:
        t //= 2
    return t


def _params(*sem):
    return pltpu.CompilerParams(dimension_semantics=sem, vmem_limit_bytes=VMEM_LIMIT)


def _dot(a, b):
    return jnp.dot(a.astype(BF16), b.astype(BF16), preferred_element_type=F32)


def _dot_nt(a, b):
    return lax.dot_general(a.astype(BF16), b.astype(BF16), (((1,), (1,)), ((), ())),
                           preferred_element_type=F32)


def _dot_tn(a, b):
    return lax.dot_general(a.astype(BF16), b.astype(BF16), (((0,), (0,)), ((), ())),
                           preferred_element_type=F32)


def _split3(x):
    hi = x.astype(BF16)
    r1 = x - hi.astype(F32)
    mid = r1.astype(BF16)
    lo = (r1 - mid.astype(F32)).astype(BF16)
    return hi, mid, lo


def _silu(x):
    return x * jax.nn.sigmoid(x)


def _norm_mm_kernel(*refs, n_streams, n_out, epilogue, has_small, row_chunk):
    x_ref, g_ref = refs[0], refs[1]
    w_refs = refs[2:2 + n_streams]
    pos = 2 + n_streams
    ws_ref = refs[pos] if has_small else None
    pos += int(has_small)
    out_refs = refs[pos:pos + n_out]
    pos += n_out
    small_out_ref = refs[pos] if has_small else None
    h_ref = refs[-1]
    tm = x_ref.shape[0]

    @pl.when(pl.program_id(1) == 0)
    def _():
        for r in range(0, tm, row_chunk):
            xf = x_ref[r:r + row_chunk, :]
            ms = jnp.mean(xf * xf, axis=-1, keepdims=True)
            h_ref[r:r + row_chunk, :] = (xf * lax.rsqrt(ms + EPS) * g_ref[...]).astype(BF16)
        if has_small:
            small_out_ref[...] = jnp.dot(h_ref[...], ws_ref[...], preferred_element_type=F32)

    h = h_ref[...]
    accs = [jnp.dot(h, w[...], preferred_element_type=F32) for w in w_refs]
    outs = epilogue(*accs)
    for o_ref, o in zip(out_refs, outs):
        o_ref[...] = o.astype(o_ref.dtype)


def norm_matmul(x, gain, w, *, name, n_streams, epilogue, n_out, out_dtype, tm, tn, w_small=None):
    t, k = x.shape
    n = w.shape[1] // n_streams
    tm, tn = _tile(t, tm), _tile(n, tn)
    nj = n // tn
    has_small = w_small is not None
    in_specs = [pl.BlockSpec((tm, k), lambda i, j: (i, 0)),
                pl.BlockSpec((1, k), lambda i, j: (0, 0))]
    args = [x, gain.reshape(1, k).astype(F32)]
    for s in range(n_streams):
        in_specs.append(pl.BlockSpec((k, tn), lambda i, j, s=s: (0, j + s * nj)))
        args.append(w)
    out_shape = [jax.ShapeDtypeStruct((t, n), out_dtype) for _ in range(n_out)]
    out_specs = [pl.BlockSpec((tm, tn), lambda i, j: (i, j)) for _ in range(n_out)]
    if has_small:
        ns = w_small.shape[1]
        in_specs.append(pl.BlockSpec((k, ns), lambda i, j: (0, 0)))
        args.append(w_small)
        out_shape.append(jax.ShapeDtypeStruct((t, ns), F32))
        out_specs.append(pl.BlockSpec((tm, ns), lambda i, j: (i, 0)))
    body = functools.partial(_norm_mm_kernel, n_streams=n_streams, n_out=n_out, epilogue=epilogue,
                             has_small=has_small, row_chunk=_tile(tm, 128))
    return pl.pallas_call(
        body,
        grid=(t // tm, nj),
        in_specs=in_specs,
        out_specs=out_specs,
        out_shape=out_shape,
        scratch_shapes=[pltpu.VMEM((tm, k), BF16)],
        compiler_params=_params("parallel", "arbitrary"),
        name=name,
    )(*args)


def _swiglu_epilogue(gate, up):
    return (_silu(gate) * up,)


def _identity_epilogue(acc):
    return (acc,)


def _shortconv_epilogue(gb, gc, xv):
    return gb, gc * xv


def _mm_residual_kernel(a_ref, w_ref, x_ref, o_ref, *, scale):
    acc = jnp.dot(a_ref[...], w_ref[...], preferred_element_type=F32)
    o_ref[...] = x_ref[...] + scale * acc


def matmul_residual(a, w, x, *, name, scale, tm, tn):
    t, k = a.shape
    n = w.shape[1]
    tm, tn = _tile(t, tm), _tile(n, tn)
    return pl.pallas_call(
        functools.partial(_mm_residual_kernel, scale=scale),
        grid=(t // tm, n // tn),
        in_specs=[pl.BlockSpec((tm, k), lambda i, j: (i, 0)),
                  pl.BlockSpec((k, tn), lambda i, j: (0, j)),
                  pl.BlockSpec((tm, tn), lambda i, j: (i, j))],
        out_specs=pl.BlockSpec((tm, tn), lambda i, j: (i, j)),
        out_shape=jax.ShapeDtypeStruct((t, n), F32),
        compiler_params=_params("parallel", "parallel"),
        name=name,
    )(a, w, x)


def _causal_conv(buf_ref, src, w_ref, seq_step):
    ts = src.shape[0]
    width = w_ref.shape[0]

    @pl.when(seq_step == 0)
    def _():
        buf_ref[0:CONV_HALO, :] = jnp.zeros((CONV_HALO, buf_ref.shape[1]), F32)

    buf_ref[CONV_HALO:CONV_HALO + ts, :] = src
    acc = None
    for tap in range(width):
        start = CONV_HALO - (width - 1) + tap
        term = buf_ref[start:start + ts, :] * w_ref[tap:tap + 1, :]
        acc = term if acc is None else acc + term
    buf_ref[0:CONV_HALO, :] = buf_ref[ts:ts + CONV_HALO, :]
    return acc


def _shortconv_gate_kernel(gb_ref, u_ref, w_ref, o_ref, buf_ref):
    conv = _causal_conv(buf_ref, u_ref[0].astype(F32), w_ref, pl.program_id(2))
    o_ref[0] = (gb_ref[0].astype(F32) * conv).astype(o_ref.dtype)


def shortconv_gate(gb, u, conv_w, *, ts, tc):
    b, s, d = gb.shape
    ts, tc = _tile(s, ts), _tile(d, tc)
    assert conv_w.shape[0] - 1 <= CONV_HALO
    blk = pl.BlockSpec((1, ts, tc), lambda bi, ci, si: (bi, si, ci))
    return pl.pallas_call(
        _shortconv_gate_kernel,
        grid=(b, d // tc, s // ts),
        in_specs=[blk, blk, pl.BlockSpec((conv_w.shape[0], tc), lambda bi, ci, si: (0, ci))],
        out_specs=blk,
        out_shape=jax.ShapeDtypeStruct((b, s, d), BF16),
        scratch_shapes=[pltpu.VMEM((CONV_HALO + ts, tc), F32)],
        compiler_params=_params("parallel", "parallel", "arbitrary"),
        name="shortconv_gate",
    )(gb, u, conv_w)


def _gdn_gates_kernel(a_ref, bt_ref, alog_ref, dtb_ref, gc_ref, beta_ref):
    c = GDN_CHUNK
    xs = a_ref[...] + dtb_ref[...]
    softplus = jnp.maximum(xs, 0.0) + jnp.log(1.0 + jnp.exp(-jnp.abs(xs)))
    g = -jnp.exp(alog_ref[...]) * softplus
    row = lax.broadcasted_iota(jnp.int32, (c, c), 0)
    col = lax.broadcasted_iota(jnp.int32, (c, c), 1)
    tri = (row >= col).astype(BF16)
    for r in range(0, g.shape[0], c):
        hi, mid, lo = _split3(g[r:r + c, :])
        gc_ref[r:r + c, :] = (jnp.dot(tri, hi, preferred_element_type=F32)
                              + jnp.dot(tri, mid, preferred_element_type=F32)
                              + jnp.dot(tri, lo, preferred_element_type=F32))
    beta_ref[...] = jax.nn.sigmoid(bt_ref[...])


def gdn_gates(ab, a_log, dt_bias, *, tb):
    t = ab.shape[0]
    h = a_log.shape[0]
    tb = _tile(t, tb)
    pad = lambda v: jnp.pad(v.astype(F32), (0, LANES - h)).reshape(1, LANES)
    row_blk = lambda j: pl.BlockSpec((tb, LANES), lambda i, j=j: (i, j))
    vec = pl.BlockSpec((1, LANES), lambda i: (0, 0))
    return pl.pallas_call(
        _gdn_gates_kernel,
        grid=(t // tb,),
        in_specs=[row_blk(0), row_blk(1), vec, vec],
        out_specs=[row_blk(0), row_blk(0)],
        out_shape=[jax.ShapeDtypeStruct((t, LANES), F32)] * 2,
        compiler_params=_params("parallel"),
        name="gdn_gates",
    )(ab, ab, pad(a_log), pad(dt_bias))


def _gdn_chunk_kernel(q_ref, k_ref, v_ref, z_ref, wq_ref, wk_ref, wv_ref, gcc_ref, bc_ref, gcr_ref,
                      onorm_ref, o_ref, bq_ref, bk_ref, bv_ref, state_ref, *, heads_per_step):
    c = GDN_CHUNK
    dk = HEAD_DIM
    seq_step = pl.program_id(2)
    ts = q_ref.shape[1]

    @pl.when(seq_step == 0)
    def _():
        state_ref[...] = jnp.zeros(state_ref.shape, F32)

    yq = _silu(_causal_conv(bq_ref, q_ref[0].astype(F32), wq_ref, seq_step))
    yk = _silu(_causal_conv(bk_ref, k_ref[0].astype(F32), wk_ref, seq_step))
    yv = _silu(_causal_conv(bv_ref, v_ref[0].astype(F32), wv_ref, seq_step))

    row = lax.broadcasted_iota(jnp.int32, (c, c), 0)
    col = lax.broadcasted_iota(jnp.int32, (c, c), 1)
    incl = row >= col
    strict = row > col
    eye = (row == col).astype(F32)
    lane = lax.broadcasted_iota(jnp.int32, (ts, LANES), 1)

    for i in range(heads_per_step):
        head = pl.program_id(1) * heads_per_step + i
        hl = slice(i * dk, (i + 1) * dk)
        qh, kh, vh = yq[:, hl], yk[:, hl], yv[:, hl]
        qh = qh * lax.rsqrt(jnp.sum(qh * qh, axis=-1, keepdims=True) + EPS) * (dk ** -0.5)
        kh = kh * lax.rsqrt(jnp.sum(kh * kh, axis=-1, keepdims=True) + EPS)
        pick = lane == head
        gcol = jnp.sum(jnp.where(pick, gcc_ref[0], 0.0), axis=-1, keepdims=True)
        bcol = jnp.sum(jnp.where(pick, bc_ref[0], 0.0), axis=-1, keepdims=True)
        grow = gcr_ref[0, pl.ds(head, 1), :]
        zh = z_ref[0, :, hl].astype(F32)
        state = state_ref[i]
        for ci in range(ts // c):
            sl = slice(ci * c, (ci + 1) * c)
            qc, kc, vc = qh[sl], kh[sl], vh[sl]
            gcl, bcl, grw = gcol[sl], bcol[sl], grow[:, sl]
            gamma = jnp.where(incl, jnp.exp(jnp.where(incl, gcl - grw, 0.0)), 0.0)
            kb = kc * bcl
            a = jnp.where(strict, _dot_nt(kb, kc) * gamma, 0.0)
            n_pow = -a
            t_inv = eye + n_pow
            for _ in range(5):
                n_pow = _dot(n_pow, n_pow)
                t_inv = t_inv + _dot(t_inv, n_pow)
            e_g = jnp.exp(gcl)
            w = _dot(t_inv, kb * e_g)
            u = _dot(t_inv, vc * bcl)
            qk = jnp.where(incl, _dot_nt(qc, kc) * gamma, 0.0)
            g_last = grw[:, c - 1:c]
            q_dec = qc * e_g
            k_dec = kc * jnp.exp(g_last - gcl)
            v_new = u - _dot(w, state)
            o = _dot(q_dec, state) + _dot(qk, v_new)
            state = state * jnp.exp(g_last) + _dot_tn(k_dec, v_new)
            o = o * lax.rsqrt(jnp.mean(o * o, axis=-1, keepdims=True) + EPS) * onorm_ref[...]
            o_ref[0, sl, hl] = (o * _silu(zh[sl])).astype(o_ref.dtype)
        state_ref[i] = state


def gdn_chunk(proj, conv_w, gcc, betac, gcr, o_norm, *, n_heads, ts, heads_per_step):
    b, s, _ = proj.shape
    dk = HEAD_DIM
    ts = _tile(s, ts)
    hb = _tile(n_heads, heads_per_step)
    groups = n_heads // hb
    assert ts % GDN_CHUNK == 0 and conv_w.shape[0] - 1 <= CONV_HALO
    width = conv_w.shape[0]
    tok = lambda part: pl.BlockSpec((1, ts, hb * dk), lambda bi, hi, si, p=part: (bi, si, hi + p * groups))
    cw = lambda part: pl.BlockSpec((width, hb * dk), lambda bi, hi, si, p=part: (0, hi + p * groups))
    gate_col = pl.BlockSpec((1, ts, LANES), lambda bi, hi, si: (bi, si, 0))
    return pl.pallas_call(
        functools.partial(_gdn_chunk_kernel, heads_per_step=hb),
        grid=(b, groups, s // ts),
        in_specs=[tok(0), tok(1), tok(2), tok(3), cw(0), cw(1), cw(2), gate_col, gate_col,
                  pl.BlockSpec((1, n_heads, ts), lambda bi, hi, si: (bi, 0, si)),
                  pl.BlockSpec((1, dk), lambda bi, hi, si: (0, 0))],
        out_specs=pl.BlockSpec((1, ts, hb * dk), lambda bi, hi, si: (bi, si, hi)),
        out_shape=jax.ShapeDtypeStruct((b, s, n_heads * dk), BF16),
        scratch_shapes=[pltpu.VMEM((CONV_HALO + ts, hb * dk), F32)] * 3
                       + [pltpu.VMEM((hb, dk, dk), F32)],
        compiler_params=_params("parallel", "parallel", "arbitrary"),
        name="gdn_chunk",
    )(proj, proj, proj, proj, conv_w, conv_w, conv_w, gcc, betac, gcr,
      o_norm.reshape(1, dk).astype(F32))


def gated_deltanet(x, gain, w_in, conv_w, a_log, dt_bias, o_norm, w_out, batch):
    t, d = x.shape
    n_heads = a_log.shape[0]
    s = t // batch
    wide = 4 * n_heads * HEAD_DIM
    w_big = w_in[:, :wide].astype(BF16)
    w_a = jnp.pad(w_in[:, wide:wide + n_heads], ((0, 0), (0, LANES - n_heads)))
    w_b = jnp.pad(w_in[:, wide + n_heads:], ((0, 0), (0, LANES - n_heads)))
    w_small = jnp.concatenate([w_a, w_b], axis=1).astype(BF16)
    proj, ab = norm_matmul(x, gain, w_big, name="gdn_in_proj", n_streams=1, epilogue=_identity_epilogue, n_out=1,
                           out_dtype=BF16, tm=512, tn=1024, w_small=w_small)
    gcc, betac = gdn_gates(ab, a_log, dt_bias, tb=512)
    gcc = gcc.reshape(batch, s, LANES)
    betac = betac.reshape(batch, s, LANES)
    gcr = jnp.swapaxes(gcc[:, :, :n_heads], 1, 2)
    o = gdn_chunk(proj.reshape(batch, s, wide), conv_w.astype(F32), gcc, betac, gcr, o_norm,
                  n_heads=n_heads, ts=256, heads_per_step=2)
    return matmul_residual(o.reshape(t, n_heads * HEAD_DIM), w_out.astype(BF16), x, name="gdn_out_proj",
                           scale=1.0, tm=512, tn=512)


def short_conv(x, gain, w_in, conv_w, w_out, batch):
    t, d = x.shape
    gb, u = norm_matmul(x, gain, w_in.astype(BF16), name="sc_in_proj", n_streams=3, epilogue=_shortconv_epilogue,
                        n_out=2, out_dtype=BF16, tm=512, tn=256)
    s = t // batch
    y = shortconv_gate(gb.reshape(batch, s, d), u.reshape(batch, s, d), conv_w.astype(F32),
                       ts=512, tc=512)
    return matmul_residual(y.reshape(t, d), w_out.astype(BF16), x, name="sc_out_proj", scale=1.0,
                           tm=512, tn=512)


def _rope_table_kernel(pos_ref, freq_ref, cos_ref, sin_ref):
    ang = pos_ref[0].astype(F32) * freq_ref[...]
    lane = lax.broadcasted_iota(jnp.int32, ang.shape, 1)
    sign = jnp.where(lane < HEAD_DIM // 2, -1.0, 1.0)
    cos_ref[0] = jnp.cos(ang)
    sin_ref[0] = jnp.sin(ang) * sign


def rope_tables(positions, *, ts):
    b, s = positions.shape
    ts = _tile(s, ts)
    half = HEAD_DIM // 2
    inv_freq = ROPE_THETA ** (-jnp.arange(half, dtype=F32) / half)
    freq = jnp.concatenate([inv_freq, inv_freq]).reshape(1, HEAD_DIM)
    blk = pl.BlockSpec((1, ts, HEAD_DIM), lambda bi, si: (bi, si, 0))
    return pl.pallas_call(
        _rope_table_kernel,
        grid=(b, s // ts),
        in_specs=[pl.BlockSpec((1, ts, 1), lambda bi, si: (bi, si, 0)),
                  pl.BlockSpec((1, HEAD_DIM), lambda bi, si: (0, 0))],
        out_specs=[blk, blk],
        out_shape=[jax.ShapeDtypeStruct((b, s, HEAD_DIM), F32)] * 2,
        compiler_params=_params("parallel", "parallel"),
        name="rope_tables",
    )(positions.reshape(b, s, 1), freq)


def _moba_prep_kernel(q_ref, k_ref, v_ref, cos_ref, sin_ref, qn_ref, kn_ref,
                      qo_ref, ko_ref, vto_ref, km_ref, *, heads_per_step):
    dh = HEAD_DIM
    cos, sin = cos_ref[0], sin_ref[0]

    def norm_rope(xh, gain):
        xh = xh * lax.rsqrt(jnp.mean(xh * xh, axis=-1, keepdims=True) + EPS) * gain
        return xh * cos + pltpu.roll(xh, dh // 2, axis=1) * sin

    for i in range(heads_per_step):
        hl = slice(i * dh, (i + 1) * dh)
        qh = norm_rope(q_ref[0, :, hl].astype(F32), qn_ref[...])
        kh = norm_rope(k_ref[0, :, hl].astype(F32), kn_ref[...])
        qo_ref[0, :, hl] = (qh * (dh ** -0.5)).astype(qo_ref.dtype)
        ko_ref[0, i, 0] = kh.astype(ko_ref.dtype)
        km_ref[0, i, 0] = jnp.mean(kh, axis=0, keepdims=True)
        vto_ref[0, i, 0] = v_ref[0, :, hl].astype(F32).T.astype(vto_ref.dtype)


def moba_prep(qkv, cos, sin, q_norm, k_norm, *, n_heads, heads_per_step):
    b, s, _ = qkv.shape
    dh, bs = HEAD_DIM, MOBA_BLOCK
    assert s % bs == 0
    nb = s // bs
    hb = _tile(n_heads, heads_per_step)
    groups = n_heads // hb
    tok = lambda part: pl.BlockSpec((1, bs, hb * dh), lambda bi, si, hi, p=part: (bi, si, hi + p * groups))
    tab = pl.BlockSpec((1, bs, dh), lambda bi, si, hi: (bi, si, 0))
    vec = pl.BlockSpec((1, dh), lambda bi, si, hi: (0, 0))
    return pl.pallas_call(
        functools.partial(_moba_prep_kernel, heads_per_step=hb),
        grid=(b, nb, groups),
        in_specs=[tok(0), tok(1), tok(2), tab, tab, vec, vec],
        out_specs=[pl.BlockSpec((1, bs, hb * dh), lambda bi, si, hi: (bi, si, hi)),
                   pl.BlockSpec((1, hb, 1, bs, dh), lambda bi, si, hi: (bi, hi, si, 0, 0)),
                   pl.BlockSpec((1, hb, 1, dh, bs), lambda bi, si, hi: (bi, hi, si, 0, 0)),
                   pl.BlockSpec((1, hb, 1, 1, dh), lambda bi, si, hi: (bi, hi, si, 0, 0))],
        out_shape=[jax.ShapeDtypeStruct((b, s, n_heads * dh), BF16),
                   jax.ShapeDtypeStruct((b, n_heads, nb, bs, dh), BF16),
                   jax.ShapeDtypeStruct((b, n_heads, nb, dh, bs), BF16),
                   jax.ShapeDtypeStruct((b, n_heads, nb, 1, dh), F32)],
        compiler_params=_params("parallel", "parallel", "parallel"),
        name="moba_prep",
    )(qkv, qkv, qkv, cos, sin, q_norm.reshape(1, dh).astype(F32), k_norm.reshape(1, dh).astype(F32))


def _moba_attn_kernel(q_ref, k_ref, vt_ref, km_ref, o_ref, bias_ref):
    bs, dh = MOBA_BLOCK, HEAD_DIM
    nb = km_ref.shape[2]
    qi = pl.program_id(2)
    q = q_ref[0]

    gate = None
    for part in _split3(km_ref[0, 0]):
        term = _dot_nt(part, q)
        gate = term if gate is None else gate + term
    blk = lax.broadcasted_iota(jnp.int32, (nb, bs), 0)
    past = blk < qi
    gate = jnp.where(past, gate, -jnp.inf)
    rank = jnp.zeros((nb, bs), jnp.int32)
    for j in range(nb):
        other = gate[j:j + 1, :]
        ahead = (other > gate) | ((other == gate) & (j < blk))
        rank = rank + ahead.astype(jnp.int32)
    bias_ref[...] = jnp.where(past & (rank < MOBA_TOPK), 0.0, -jnp.inf)

    key_pos = lax.broadcasted_iota(jnp.int32, (bs, bs), 0)
    qry_pos = lax.broadcasted_iota(jnp.int32, (bs, bs), 1)
    s_t = jnp.where(key_pos <= qry_pos, _dot_nt(k_ref[0, 0, qi], q), -jnp.inf)
    m = jnp.max(s_t, axis=0, keepdims=True)
    p = jnp.exp(s_t - m)
    l = jnp.sum(p, axis=0, keepdims=True)
    acc = _dot(vt_ref[0, 0, qi], p)

    def body(j, carry):
        m, l, acc = carry
        s_t = _dot_nt(k_ref[0, 0, j], q) + bias_ref[pl.ds(j, 1), :]
        m_new = jnp.maximum(m, jnp.max(s_t, axis=0, keepdims=True))
        alpha = jnp.exp(m - m_new)
        p = jnp.exp(s_t - m_new)
        l = alpha * l + jnp.sum(p, axis=0, keepdims=True)
        acc = alpha * acc + _dot(vt_ref[0, 0, j], p)
        return m_new, l, acc

    m, l, acc = lax.fori_loop(0, qi, body, (m, l, acc))
    o_ref[0] = (acc / l).T.astype(o_ref.dtype)


def moba_attn(q, k, vt, kmean):
    b, s, d = q.shape
    _, n_heads, nb, bs, dh = k.shape
    return pl.pallas_call(
        _moba_attn_kernel,
        grid=(b, n_heads, nb),
        in_specs=[pl.BlockSpec((1, bs, dh), lambda bi, hi, qi: (bi, qi, hi)),
                  pl.BlockSpec((1, 1, nb, bs, dh), lambda bi, hi, qi: (bi, hi, 0, 0, 0)),
                  pl.BlockSpec((1, 1, nb, dh, bs), lambda bi, hi, qi: (bi, hi, 0, 0, 0)),
                  pl.BlockSpec((1, 1, nb, dh), lambda bi, hi, qi: (bi, hi, 0, 0))],
        out_specs=pl.BlockSpec((1, bs, dh), lambda bi, hi, qi: (bi, qi, hi)),
        out_shape=jax.ShapeDtypeStruct((b, s, d), BF16),
        scratch_shapes=[pltpu.VMEM((nb, bs), F32)],
        compiler_params=_params("parallel", "parallel", "arbitrary"),
        name="moba_attn",
    )(q, k, vt, kmean)


def moba_attention(x, gain, positions, w_in, q_norm, k_norm, w_out):
    t, d = x.shape
    batch, s = positions.shape
    n_heads = d // HEAD_DIM
    (qkv,) = norm_matmul(x, gain, w_in.astype(BF16), name="moba_in_proj", n_streams=1, epilogue=_identity_epilogue,
                         n_out=1, out_dtype=BF16, tm=512, tn=1024)
    cos, sin = rope_tables(positions, ts=512)
    q, k, vt, kmean = moba_prep(qkv.reshape(batch, s, 3 * d), cos, sin, q_norm, k_norm,
                                n_heads=n_heads, heads_per_step=4)
    nb = s // MOBA_BLOCK
    o = moba_attn(q, k, vt, kmean.reshape(batch, n_heads, nb, HEAD_DIM))
    return matmul_residual(o.reshape(t, d), w_out.astype(BF16), x, name="moba_out_proj", scale=1.0,
                           tm=512, tn=512)


def swiglu_half_step(x, gain, w_in, w_out):
    (act,) = norm_matmul(x, gain, w_in.astype(BF16), name="ffn_in", n_streams=2, epilogue=_swiglu_epilogue,
                         n_out=1, out_dtype=BF16, tm=512, tn=512)
    return matmul_residual(act, w_out.astype(BF16), x, name="ffn_out", scale=0.5, tm=512, tn=512)


def kernel(x, positions, norm_ffn1, norm_mix, norm_ffn2, ffn1_w_in, ffn1_w_out, ffn2_w_in, ffn2_w_out, gdn_w_in, gdn_conv_w, gdn_a_log, gdn_dt_bias, gdn_out_norm, gdn_w_out, sc_w_in, sc_conv_w, sc_w_out, moba_w_in, moba_q_norm, moba_k_norm, moba_w_out):
    batch, s, d = x.shape
    depth = norm_ffn1.shape[0]
    xt = x.reshape(batch * s, d)
    for i in range(depth):
        kind, j = i % 3, i // 3
        xt = swiglu_half_step(xt, norm_ffn1[i], ffn1_w_in[i], ffn1_w_out[i])
        if kind == 0:
            xt = gated_deltanet(xt, norm_mix[i], gdn_w_in[j], gdn_conv_w[j], gdn_a_log[j],
                                gdn_dt_bias[j], gdn_out_norm[j], gdn_w_out[j], batch)
        elif kind == 1:
            xt = short_conv(xt, norm_mix[i], sc_w_in[j], sc_conv_w[j], sc_w_out[j], batch)
        else:
            xt = moba_attention(xt, norm_mix[i], positions, moba_w_in[j], moba_q_norm[j],
                                moba_k_norm[j], moba_w_out[j])
        xt = swiglu_half_step(xt, norm_ffn2[i], ffn2_w_in[i], ffn2_w_out[i])
    return xt.reshape(batch, s, d)
```

```python
import functools

import jax
import jax.numpy as jnp
from jax import lax
from jax.experimental import pallas as pl
from jax.experimental.pallas import tpu as pltpu

F32 = jnp.float32
BF16 = jnp.bfloat16

EPS = 1e-6
HEAD_DIM = 128
GDN_CHUNK = 64
MOBA_BLOCK = 256
MOBA_TOPK = 3
ROPE_THETA = 10000.0

LANES = 128
CONV_HALO = 8
VMEM_LIMIT = 56 * 1024 * 1024


def _tile(dim, want):
    t = min(dim, want)
    while dim % t:
        t //= 2
    return t


def _params(*sem):
    return pltpu.CompilerParams(dimension_semantics=sem, vmem_limit_bytes=VMEM_LIMIT)


def _dot(a, b):
    return jnp.dot(a.astype(BF16), b.astype(BF16), preferred_element_type=F32)


def _dot_nt(a, b):
    return lax.dot_general(a.astype(BF16), b.astype(BF16), (((1,), (1,)), ((), ())),
                           preferred_element_type=F32)


def _dot_tn(a, b):
    return lax.dot_general(a.astype(BF16), b.astype(BF16), (((0,), (0,)), ((), ())),
                           preferred_element_type=F32)


def _split3(x):
    hi = x.astype(BF16)
    r1 = x - hi.astype(F32)
    mid = r1.astype(BF16)
    lo = (r1 - mid.astype(F32)).astype(BF16)
    return hi, mid, lo


def _silu(x):
    return x * jax.nn.sigmoid(x)


def _norm_mm_kernel(*refs, n_streams, n_out, epilogue, has_small, row_chunk):
    x_ref, g_ref = refs[0], refs[1]
    w_refs = refs[2:2 + n_streams]
    pos = 2 + n_streams
    ws_ref = refs[pos] if has_small else None
    pos += int(has_small)
    out_refs = refs[pos:pos + n_out]
    pos += n_out
    small_out_ref = refs[pos] if has_small else None
    h_ref = refs[-1]
    tm = x_ref.shape[0]

    @pl.when(pl.program_id(1) == 0)
    def _():
        for r in range(0, tm, row_chunk):
            xf = x_ref[r:r + row_chunk, :]
            ms = jnp.mean(xf * xf, axis=-1, keepdims=True)
            h_ref[r:r + row_chunk, :] = (xf * lax.rsqrt(ms + EPS) * g_ref[...]).astype(BF16)
        if has_small:
            small_out_ref[...] = jnp.dot(h_ref[...], ws_ref[...], preferred_element_type=F32)

    h = h_ref[...]
    accs = [jnp.dot(h, w[...], preferred_element_type=F32) for w in w_refs]
    outs = epilogue(*accs)
    for o_ref, o in zip(out_refs, outs):
        o_ref[...] = o.astype(o_ref.dtype)


def norm_matmul(x, gain, w, layer, *, name, n_streams, epilogue, n_out, out_dtype, tm, tn, w_small=None):
    t, k = x.shape
    n = w.shape[2] // n_streams
    tm, tn = _tile(t, tm), _tile(n, tn)
    nj = n // tn
    has_small = w_small is not None
    in_specs = [pl.BlockSpec((tm, k), lambda i, j: (i, 0)),
                pl.BlockSpec((1, k), lambda i, j: (0, 0))]
    args = [x, gain.reshape(1, k).astype(F32)]
    for s in range(n_streams):
        in_specs.append(pl.BlockSpec((pl.Squeezed(), k, tn), lambda i, j, s=s: (layer, 0, j + s * nj)))
        args.append(w)
    out_shape = [jax.ShapeDtypeStruct((t, n), out_dtype) for _ in range(n_out)]
    out_specs = [pl.BlockSpec((tm, tn), lambda i, j: (i, j)) for _ in range(n_out)]
    if has_small:
        ns = w_small.shape[2]
        in_specs.append(pl.BlockSpec((pl.Squeezed(), k, ns), lambda i, j: (layer, 0, 0)))
        args.append(w_small)
        out_shape.append(jax.ShapeDtypeStruct((t, ns), F32))
        out_specs.append(pl.BlockSpec((tm, ns), lambda i, j: (i, 0)))
    body = functools.partial(_norm_mm_kernel, n_streams=n_streams, n_out=n_out, epilogue=epilogue,
                             has_small=has_small, row_chunk=_tile(tm, 128))
    return pl.pallas_call(
        body,
        grid=(t // tm, nj),
        in_specs=in_specs,
        out_specs=out_specs,
        out_shape=out_shape,
        scratch_shapes=[pltpu.VMEM((tm, k), BF16)],
        compiler_params=_params("parallel", "arbitrary"),
        name=name,
    )(*args)


def _swiglu_epilogue(gate, up):
    return (_silu(gate) * up,)


def _identity_epilogue(acc):
    return (acc,)


def _shortconv_epilogue(gb, gc, xv):
    return gb, gc * xv


def _mm_residual_kernel(a_ref, w_ref, x_ref, o_ref, *, scale):
    acc = jnp.dot(a_ref[...], w_ref[...], preferred_element_type=F32)
    o_ref[...] = x_ref[...] + scale * acc


def matmul_residual(a, w, layer, x, *, name, scale, tm, tn):
    t, k = a.shape
    n = w.shape[2]
    tm, tn = _tile(t, tm), _tile(n, tn)
    return pl.pallas_call(
        functools.partial(_mm_residual_kernel, scale=scale),
        grid=(t // tm, n // tn),
        in_specs=[pl.BlockSpec((tm, k), lambda i, j: (i, 0)),
                  pl.BlockSpec((pl.Squeezed(), k, tn), lambda i, j: (layer, 0, j)),
                  pl.BlockSpec((tm, tn), lambda i, j: (i, j))],
        out_specs=pl.BlockSpec((tm, tn), lambda i, j: (i, j)),
        out_shape=jax.ShapeDtypeStruct((t, n), F32),
        compiler_params=_params("parallel", "parallel"),
        name=name,
    )(a, w, x)


def _causal_conv(buf_ref, src, w_ref, seq_step):
    ts = src.shape[0]
    width = w_ref.shape[0]

    @pl.when(seq_step == 0)
    def _():
        buf_ref[0:CONV_HALO, :] = jnp.zeros((CONV_HALO, buf_ref.shape[1]), F32)

    buf_ref[CONV_HALO:CONV_HALO + ts, :] = src
    acc = None
    for tap in range(width):
        start = CONV_HALO - (width - 1) + tap
        term = buf_ref[start:start + ts, :] * w_ref[tap:tap + 1, :]
        acc = term if acc is None else acc + term
    buf_ref[0:CONV_HALO, :] = buf_ref[ts:ts + CONV_HALO, :]
    return acc


def _shortconv_gate_kernel(gb_ref, u_ref, w_ref, o_ref, buf_ref):
    conv = _causal_conv(buf_ref, u_ref[0].astype(F32), w_ref, pl.program_id(2))
    o_ref[0] = (gb_ref[0].astype(F32) * conv).astype(o_ref.dtype)


def shortconv_gate(gb, u, conv_w, *, ts, tc):
    b, s, d = gb.shape
    ts, tc = _tile(s, ts), _tile(d, tc)
    assert conv_w.shape[0] - 1 <= CONV_HALO
    blk = pl.BlockSpec((1, ts, tc), lambda bi, ci, si: (bi, si, ci))
    return pl.pallas_call(
        _shortconv_gate_kernel,
        grid=(b, d // tc, s // ts),
        in_specs=[blk, blk, pl.BlockSpec((conv_w.shape[0], tc), lambda bi, ci, si: (0, ci))],
        out_specs=blk,
        out_shape=jax.ShapeDtypeStruct((b, s, d), BF16),
        scratch_shapes=[pltpu.VMEM((CONV_HALO + ts, tc), F32)],
        compiler_params=_params("parallel", "parallel", "arbitrary"),
        name="shortconv_gate",
    )(gb, u, conv_w)


def _gdn_gates_kernel(a_ref, bt_ref, alog_ref, dtb_ref, gc_ref, beta_ref):
    c = GDN_CHUNK
    xs = a_ref[...] + dtb_ref[...]
    softplus = jnp.maximum(xs, 0.0) + jnp.log(1.0 + jnp.exp(-jnp.abs(xs)))
    g = -jnp.exp(alog_ref[...]) * softplus
    row = lax.broadcasted_iota(jnp.int32, (c, c), 0)
    col = lax.broadcasted_iota(jnp.int32, (c, c), 1)
    tri = (row >= col).astype(BF16)
    for r in range(0, g.shape[0], c):
        hi, mid, lo = _split3(g[r:r + c, :])
        gc_ref[r:r + c, :] = (jnp.dot(tri, hi, preferred_element_type=F32)
                              + jnp.dot(tri, mid, preferred_element_type=F32)
                              + jnp.dot(tri, lo, preferred_element_type=F32))
    beta_ref[...] = jax.nn.sigmoid(bt_ref[...])


def gdn_gates(ab, a_log, dt_bias, *, tb):
    t = ab.shape[0]
    h = a_log.shape[0]
    tb = _tile(t, tb)
    pad = lambda v: jnp.pad(v.astype(F32), (0, LANES - h)).reshape(1, LANES)
    row_blk = lambda j: pl.BlockSpec((tb, LANES), lambda i, j=j: (i, j))
    vec = pl.BlockSpec((1, LANES), lambda i: (0, 0))
    return pl.pallas_call(
        _gdn_gates_kernel,
        grid=(t // tb,),
        in_specs=[row_blk(0), row_blk(1), vec, vec],
        out_specs=[row_blk(0), row_blk(0)],
        out_shape=[jax.ShapeDtypeStruct((t, LANES), F32)] * 2,
        compiler_params=_params("parallel"),
        name="gdn_gates",
    )(ab, ab, pad(a_log), pad(dt_bias))


def _gdn_chunk_kernel(q_ref, k_ref, v_ref, z_ref, wq_ref, wk_ref, wv_ref, gcc_ref, bc_ref, gcr_ref,
                      onorm_ref, o_ref, bq_ref, bk_ref, bv_ref, state_ref, *, heads_per_step):
    c = GDN_CHUNK
    dk = HEAD_DIM
    seq_step = pl.program_id(2)
    ts = q_ref.shape[1]

    @pl.when(seq_step == 0)
    def _():
        state_ref[...] = jnp.zeros(state_ref.shape, F32)

    yq = _silu(_causal_conv(bq_ref, q_ref[0].astype(F32), wq_ref, seq_step))
    yk = _silu(_causal_conv(bk_ref, k_ref[0].astype(F32), wk_ref, seq_step))
    yv = _silu(_causal_conv(bv_ref, v_ref[0].astype(F32), wv_ref, seq_step))

    row = lax.broadcasted_iota(jnp.int32, (c, c), 0)
    col = lax.broadcasted_iota(jnp.int32, (c, c), 1)
    incl = row >= col
    strict = row > col
    eye = (row == col).astype(F32)
    lane = lax.broadcasted_iota(jnp.int32, (ts, LANES), 1)
    hb, nc = heads_per_step, ts // c

    per_head = []
    for i in range(hb):
        head = pl.program_id(1) * hb + i
        hl = slice(i * dk, (i + 1) * dk)
        qh, kh, vh = yq[:, hl], yk[:, hl], yv[:, hl]
        qh = qh * lax.rsqrt(jnp.sum(qh * qh, axis=-1, keepdims=True) + EPS) * (dk ** -0.5)
        kh = kh * lax.rsqrt(jnp.sum(kh * kh, axis=-1, keepdims=True) + EPS)
        pick = lane == head
        gcol = jnp.sum(jnp.where(pick, gcc_ref[0], 0.0), axis=-1, keepdims=True)
        bcol = jnp.sum(jnp.where(pick, bc_ref[0], 0.0), axis=-1, keepdims=True)
        grow = gcr_ref[0, pl.ds(head, 1), :]
        per_head.append((qh, kh, vh, gcol, bcol, grow))
    items = [(ci, i) for ci in range(nc) for i in range(hb)]
    qs, ks, vs, gcols, bcols, grows = [], [], [], [], [], []
    for ci, i in items:
        qh, kh, vh, gcol, bcol, grow = per_head[i]
        sl = slice(ci * c, (ci + 1) * c)
        qs.append(qh[sl])
        ks.append(kh[sl])
        vs.append(vh[sl])
        gcols.append(gcol[sl])
        bcols.append(bcol[sl])
        grows.append(grow[:, sl])
    n = len(items)
    rng = range(n)
    gammas = [jnp.where(incl, jnp.exp(jnp.where(incl, gcols[t] - grows[t], 0.0)), 0.0) for t in rng]
    kbs = [ks[t] * bcols[t] for t in rng]
    n_pows = [jnp.where(strict, -(_dot_nt(kbs[t], ks[t]) * gammas[t]), 0.0) for t in rng]
    t_invs = [eye + n_pows[t] for t in rng]
    for _ in range(5):
        n_pows = [_dot(n_pows[t], n_pows[t]) for t in rng]
        t_invs = [t_invs[t] + _dot(t_invs[t], n_pows[t]) for t in rng]
    e_gs = [jnp.exp(gcols[t]) for t in rng]
    wus = [_dot(t_invs[t], jnp.concatenate([kbs[t] * e_gs[t], vs[t] * bcols[t]], axis=1)) for t in rng]
    qks = [jnp.where(incl, _dot_nt(qs[t], ks[t]) * gammas[t], 0.0) for t in rng]
    g_lasts = [grows[t][:, c - 1:c] for t in rng]
    q_decs = [qs[t] * e_gs[t] for t in rng]
    k_decs = [ks[t] * jnp.exp(g_lasts[t] - gcols[t]) for t in rng]
    decays = [jnp.exp(g_lasts[t]) for t in rng]

    states = [state_ref[i] for i in range(hb)]
    outs = [None] * n
    for ci in range(nc):
        idx = [ci * hb + i for i in range(hb)]
        v_news = [wus[t][:, dk:] - _dot(wus[t][:, :dk], states[i]) for i, t in enumerate(idx)]
        o_state = [_dot(q_decs[t], states[i]) for i, t in enumerate(idx)]
        for i, t in enumerate(idx):
            outs[t] = o_state[i] + _dot(qks[t], v_news[i])
        states = [states[i] * decays[t] + _dot_tn(k_decs[t], v_news[i]) for i, t in enumerate(idx)]
    for i in range(hb):
        state_ref[i] = states[i]

    for t, (ci, i) in enumerate(items):
        o = outs[t]
        o = o * lax.rsqrt(jnp.mean(o * o, axis=-1, keepdims=True) + EPS) * onorm_ref[...]
        z = z_ref[0, ci * c:(ci + 1) * c, i * dk:(i + 1) * dk].astype(F32)
        o_ref[0, ci * c:(ci + 1) * c, i * dk:(i + 1) * dk] = (o * _silu(z)).astype(o_ref.dtype)


def gdn_chunk(proj, conv_w, gcc, betac, gcr, o_norm, *, n_heads, ts, heads_per_step):
    b, s, _ = proj.shape
    dk = HEAD_DIM
    ts = _tile(s, ts)
    hb = _tile(n_heads, heads_per_step)
    groups = n_heads // hb
    assert ts % GDN_CHUNK == 0 and conv_w.shape[0] - 1 <= CONV_HALO
    width = conv_w.shape[0]
    tok = lambda part: pl.BlockSpec((1, ts, hb * dk), lambda bi, hi, si, p=part: (bi, si, hi + p * groups))
    cw = lambda part: pl.BlockSpec((width, hb * dk), lambda bi, hi, si, p=part: (0, hi + p * groups))
    gate_col = pl.BlockSpec((1, ts, LANES), lambda bi, hi, si: (bi, si, 0))
    return pl.pallas_call(
        functools.partial(_gdn_chunk_kernel, heads_per_step=hb),
        grid=(b, groups, s // ts),
        in_specs=[tok(0), tok(1), tok(2), tok(3), cw(0), cw(1), cw(2), gate_col, gate_col,
                  pl.BlockSpec((1, n_heads, ts), lambda bi, hi, si: (bi, 0, si)),
                  pl.BlockSpec((1, dk), lambda bi, hi, si: (0, 0))],
        out_specs=pl.BlockSpec((1, ts, hb * dk), lambda bi, hi, si: (bi, si, hi)),
        out_shape=jax.ShapeDtypeStruct((b, s, n_heads * dk), BF16),
        scratch_shapes=[pltpu.VMEM((CONV_HALO + ts, hb * dk), F32)] * 3
                       + [pltpu.VMEM((hb, dk, dk), F32)],
        compiler_params=_params("parallel", "parallel", "arbitrary"),
        name="gdn_chunk",
    )(proj, proj, proj, proj, conv_w, conv_w, conv_w, gcc, betac, gcr,
      o_norm.reshape(1, dk).astype(F32))


def gdn_split_weights(w_in, n_heads):
    wide = 4 * n_heads * HEAD_DIM
    pad = ((0, 0), (0, 0), (0, LANES - n_heads))
    w_a = jnp.pad(w_in[:, :, wide:wide + n_heads], pad)
    w_b = jnp.pad(w_in[:, :, wide + n_heads:], pad)
    return w_in[:, :, :wide].astype(BF16), jnp.concatenate([w_a, w_b], axis=2).astype(BF16)


def gated_deltanet(x, gain, w_big, w_small, layer, conv_w, a_log, dt_bias, o_norm, w_out, batch):
    t, d = x.shape
    n_heads = a_log.shape[0]
    s = t // batch
    wide = 4 * n_heads * HEAD_DIM
    proj, ab = norm_matmul(x, gain, w_big, layer, name="gdn_in_proj", n_streams=1,
                           epilogue=_identity_epilogue, n_out=1, out_dtype=BF16, tm=512, tn=1024,
                           w_small=w_small)
    gcc, betac = gdn_gates(ab, a_log, dt_bias, tb=512)
    gcc = gcc.reshape(batch, s, LANES)
    betac = betac.reshape(batch, s, LANES)
    gcr = jnp.swapaxes(gcc[:, :, :n_heads], 1, 2)
    o = gdn_chunk(proj.reshape(batch, s, wide), conv_w.astype(F32), gcc, betac, gcr, o_norm,
                  n_heads=n_heads, ts=256, heads_per_step=4)
    return matmul_residual(o.reshape(t, n_heads * HEAD_DIM), w_out, layer, x, name="gdn_out_proj",
                           scale=1.0, tm=512, tn=1024)


def short_conv(x, gain, w_in, layer, conv_w, w_out, batch):
    t, d = x.shape
    gb, u = norm_matmul(x, gain, w_in, layer, name="sc_in_proj", n_streams=3,
                        epilogue=_shortconv_epilogue, n_out=2, out_dtype=BF16, tm=512, tn=256)
    s = t // batch
    y = shortconv_gate(gb.reshape(batch, s, d), u.reshape(batch, s, d), conv_w.astype(F32),
                       ts=512, tc=512)
    return matmul_residual(y.reshape(t, d), w_out, layer, x, name="sc_out_proj", scale=1.0,
                           tm=512, tn=1024)


def _rope_table_kernel(pos_ref, freq_ref, cos_ref, sin_ref):
    ang = pos_ref[0].astype(F32) * freq_ref[...]
    lane = lax.broadcasted_iota(jnp.int32, ang.shape, 1)
    sign = jnp.where(lane < HEAD_DIM // 2, -1.0, 1.0)
    cos_ref[0] = jnp.cos(ang)
    sin_ref[0] = jnp.sin(ang) * sign


def rope_tables(positions, *, ts):
    b, s = positions.shape
    ts = _tile(s, ts)
    half = HEAD_DIM // 2
    inv_freq = ROPE_THETA ** (-jnp.arange(half, dtype=F32) / half)
    freq = jnp.concatenate([inv_freq, inv_freq]).reshape(1, HEAD_DIM)
    blk = pl.BlockSpec((1, ts, HEAD_DIM), lambda bi, si: (bi, si, 0))
    return pl.pallas_call(
        _rope_table_kernel,
        grid=(b, s // ts),
        in_specs=[pl.BlockSpec((1, ts, 1), lambda bi, si: (bi, si, 0)),
                  pl.BlockSpec((1, HEAD_DIM), lambda bi, si: (0, 0))],
        out_specs=[blk, blk],
        out_shape=[jax.ShapeDtypeStruct((b, s, HEAD_DIM), F32)] * 2,
        compiler_params=_params("parallel", "parallel"),
        name="rope_tables",
    )(positions.reshape(b, s, 1), freq)


def _moba_prep_kernel(q_ref, k_ref, v_ref, cos_ref, sin_ref, qn_ref, kn_ref,
                      qo_ref, ko_ref, vto_ref, km_ref, *, heads_per_step):
    dh = HEAD_DIM
    cos, sin = cos_ref[0], sin_ref[0]

    def norm_rope(xh, gain):
        xh = xh * lax.rsqrt(jnp.mean(xh * xh, axis=-1, keepdims=True) + EPS) * gain
        return xh * cos + pltpu.roll(xh, dh // 2, axis=1) * sin

    for i in range(heads_per_step):
        hl = slice(i * dh, (i + 1) * dh)
        qh = norm_rope(q_ref[0, :, hl].astype(F32), qn_ref[...])
        kh = norm_rope(k_ref[0, :, hl].astype(F32), kn_ref[...])
        qo_ref[0, :, hl] = (qh * (dh ** -0.5)).astype(qo_ref.dtype)
        ko_ref[0, i, 0] = kh.astype(ko_ref.dtype)
        km_ref[0, i, 0] = jnp.mean(kh, axis=0, keepdims=True)
        vto_ref[0, i, 0] = v_ref[0, :, hl].astype(F32).T.astype(vto_ref.dtype)


def moba_prep(qkv, cos, sin, q_norm, k_norm, *, n_heads, heads_per_step):
    b, s, _ = qkv.shape
    dh, bs = HEAD_DIM, MOBA_BLOCK
    assert s % bs == 0
    nb = s // bs
    hb = _tile(n_heads, heads_per_step)
    groups = n_heads // hb
    tok = lambda part: pl.BlockSpec((1, bs, hb * dh), lambda bi, si, hi, p=part: (bi, si, hi + p * groups))
    tab = pl.BlockSpec((1, bs, dh), lambda bi, si, hi: (bi, si, 0))
    vec = pl.BlockSpec((1, dh), lambda bi, si, hi: (0, 0))
    return pl.pallas_call(
        functools.partial(_moba_prep_kernel, heads_per_step=hb),
        grid=(b, nb, groups),
        in_specs=[tok(0), tok(1), tok(2), tab, tab, vec, vec],
        out_specs=[pl.BlockSpec((1, bs, hb * dh), lambda bi, si, hi: (bi, si, hi)),
                   pl.BlockSpec((1, hb, 1, bs, dh), lambda bi, si, hi: (bi, hi, si, 0, 0)),
                   pl.BlockSpec((1, hb, 1, dh, bs), lambda bi, si, hi: (bi, hi, si, 0, 0)),
                   pl.BlockSpec((1, hb, 1, 1, dh), lambda bi, si, hi: (bi, hi, si, 0, 0))],
        out_shape=[jax.ShapeDtypeStruct((b, s, n_heads * dh), BF16),
                   jax.ShapeDtypeStruct((b, n_heads, nb, bs, dh), BF16),
                   jax.ShapeDtypeStruct((b, n_heads, nb, dh, bs), BF16),
                   jax.ShapeDtypeStruct((b, n_heads, nb, 1, dh), F32)],
        compiler_params=_params("parallel", "parallel", "parallel"),
        name="moba_prep",
    )(qkv, qkv, qkv, cos, sin, q_norm.reshape(1, dh).astype(F32), k_norm.reshape(1, dh).astype(F32))


def _moba_attn_kernel(q_ref, k_ref, vt_ref, km_ref, o_ref, bias_ref, *, group):
    bs, dh = MOBA_BLOCK, HEAD_DIM
    nb = km_ref.shape[2]
    qi = pl.program_id(2)
    q = q_ref[0]

    gate = None
    for part in _split3(km_ref[0, 0]):
        term = _dot_nt(part, q)
        gate = term if gate is None else gate + term
    blk = lax.broadcasted_iota(jnp.int32, (nb, bs), 0)
    past = blk < qi
    gate = jnp.where(past, gate, -jnp.inf)
    rank = jnp.zeros((nb, bs), jnp.int32)
    for j in range(nb):
        other = gate[j:j + 1, :]
        ahead = (other > gate) | ((other == gate) & (j < blk))
        rank = rank + ahead.astype(jnp.int32)
    bias_ref[...] = jnp.where(past & (rank < MOBA_TOPK), 0.0, -jnp.inf)

    def local_softmax(blocks, masks):
        s_t = [_dot_nt(k_ref[0, 0, j], q) + mask for j, mask in zip(blocks, masks)]
        m_loc = functools.reduce(jnp.maximum, [jnp.max(x, axis=0, keepdims=True) for x in s_t])
        m_safe = jnp.where(m_loc == -jnp.inf, 0.0, m_loc)
        p = [jnp.exp(x - m_safe) for x in s_t]
        l_loc = functools.reduce(jnp.add, [jnp.sum(x, axis=0, keepdims=True) for x in p])
        acc = functools.reduce(jnp.add, [_dot(vt_ref[0, 0, j], x) for j, x in zip(blocks, p)])
        return m_loc, l_loc, acc

    def merge(parts):
        m_new = functools.reduce(jnp.maximum, [m for m, _, _ in parts])
        scales = [jnp.exp(m - m_new) for m, _, _ in parts]
        l = functools.reduce(jnp.add, [w * l for w, (_, l, _) in zip(scales, parts)])
        acc = functools.reduce(jnp.add, [w * a for w, (_, _, a) in zip(scales, parts)])
        return m_new, l, acc

    key_pos = lax.broadcasted_iota(jnp.int32, (bs, bs), 0)
    qry_pos = lax.broadcasted_iota(jnp.int32, (bs, bs), 1)
    causal = jnp.where(key_pos <= qry_pos, 0.0, -jnp.inf)
    carry = local_softmax([qi], [causal])

    half = group // 2

    def body(it, carry):
        j0 = it * group
        parts = [carry]
        for h0 in (0, half):
            blocks = [j0 + h0 + b for b in range(half)]
            parts.append(local_softmax(blocks, [bias_ref[pl.ds(j, 1), :] for j in blocks]))
        return merge(parts)

    m, l, acc = lax.fori_loop(0, (qi + group - 1) // group, body, carry)
    o_ref[0] = (acc / l).T.astype(o_ref.dtype)


def moba_attn(q, k, vt, kmean):
    b, s, d = q.shape
    _, n_heads, nb, bs, dh = k.shape
    assert nb % 2 == 0
    group = 4 if nb % 4 == 0 else 2
    return pl.pallas_call(
        functools.partial(_moba_attn_kernel, group=group),
        grid=(b, n_heads, nb),
        in_specs=[pl.BlockSpec((1, bs, dh), lambda bi, hi, qi: (bi, qi, hi)),
                  pl.BlockSpec((1, 1, nb, bs, dh), lambda bi, hi, qi: (bi, hi, 0, 0, 0)),
                  pl.BlockSpec((1, 1, nb, dh, bs), lambda bi, hi, qi: (bi, hi, 0, 0, 0)),
                  pl.BlockSpec((1, 1, nb, dh), lambda bi, hi, qi: (bi, hi, 0, 0))],
        out_specs=pl.BlockSpec((1, bs, dh), lambda bi, hi, qi: (bi, qi, hi)),
        out_shape=jax.ShapeDtypeStruct((b, s, d), BF16),
        scratch_shapes=[pltpu.VMEM((nb, bs), F32)],
        compiler_params=_params("parallel", "parallel", "arbitrary"),
        name="moba_attn",
    )(q, k, vt, kmean)


def moba_attention(x, gain, positions, w_in, layer, q_norm, k_norm, w_out):
    t, d = x.shape
    batch, s = positions.shape
    n_heads = d // HEAD_DIM
    (qkv,) = norm_matmul(x, gain, w_in, layer, name="moba_in_proj", n_streams=1,
                         epilogue=_identity_epilogue, n_out=1, out_dtype=BF16, tm=512, tn=1024)
    cos, sin = rope_tables(positions, ts=512)
    q, k, vt, kmean = moba_prep(qkv.reshape(batch, s, 3 * d), cos, sin, q_norm, k_norm,
                                n_heads=n_heads, heads_per_step=4)
    nb = s // MOBA_BLOCK
    o = moba_attn(q, k, vt, kmean.reshape(batch, n_heads, nb, HEAD_DIM))
    return matmul_residual(o.reshape(t, d), w_out, layer, x, name="moba_out_proj", scale=1.0,
                           tm=512, tn=1024)


def swiglu_half_step(x, gain, w_in, w_out, layer):
    (act,) = norm_matmul(x, gain, w_in, layer, name="ffn_in", n_streams=2, epilogue=_swiglu_epilogue,
                         n_out=1, out_dtype=BF16, tm=512, tn=512)
    return matmul_residual(act, w_out, layer, x, name="ffn_out", scale=0.5, tm=512, tn=1024)


def kernel(x, positions, norm_ffn1, norm_mix, norm_ffn2, ffn1_w_in, ffn1_w_out, ffn2_w_in, ffn2_w_out, gdn_w_in, gdn_conv_w, gdn_a_log, gdn_dt_bias, gdn_out_norm, gdn_w_out, sc_w_in, sc_conv_w, sc_w_out, moba_w_in, moba_q_norm, moba_k_norm, moba_w_out):
    batch, s, d = x.shape
    depth = norm_ffn1.shape[0]
    ffn1_w_in, ffn1_w_out, ffn2_w_in, ffn2_w_out, gdn_w_out, sc_w_in, sc_w_out, moba_w_in, moba_w_out = (
        w.astype(BF16) for w in (ffn1_w_in, ffn1_w_out, ffn2_w_in, ffn2_w_out, gdn_w_out, sc_w_in,
                                 sc_w_out, moba_w_in, moba_w_out))
    gdn_w_big, gdn_w_small = gdn_split_weights(gdn_w_in, gdn_a_log.shape[1])
    xt = x.reshape(batch * s, d)
    for i in range(depth):
        kind, j = i % 3, i // 3
        xt = swiglu_half_step(xt, norm_ffn1[i], ffn1_w_in, ffn1_w_out, i)
        if kind == 0:
            xt = gated_deltanet(xt, norm_mix[i], gdn_w_big, gdn_w_small, j, gdn_conv_w[j], gdn_a_log[j],
                                gdn_dt_bias[j], gdn_out_norm[j], gdn_w_out, batch)
        elif kind == 1:
            xt = short_conv(xt, norm_mix[i], sc_w_in, j, sc_conv_w[j], sc_w_out, batch)
        else:
            xt = moba_attention(xt, norm_mix[i], positions, moba_w_in, j, moba_q_norm[j],
                                moba_k_norm[j], moba_w_out)
        xt = swiglu_half_step(xt, norm_ffn2[i], ffn2_w_in, ffn2_w_out, i)
    return xt.reshape(batch, s, d)
```

```python
import functools

import jax
import jax.numpy as jnp
from jax import lax
from jax.experimental import pallas as pl
from jax.experimental.pallas import tpu as pltpu

F32 = jnp.float32
BF16 = jnp.bfloat16

EPS = 1e-6
HEAD_DIM = 128
GDN_CHUNK = 64
MOBA_BLOCK = 256
MOBA_TOPK = 3
ROPE_THETA = 10000.0
LOG2_E = 1.4426950408889634
SUM_ROWS = 16

LANES = 128
CONV_HALO = 8
VMEM_LIMIT = 56 * 1024 * 1024


def _tile(dim, want):
    t = min(dim, want)
    while dim % t:
        t //= 2
    return t


def _params(*sem):
    return pltpu.CompilerParams(dimension_semantics=sem, vmem_limit_bytes=VMEM_LIMIT)


def _dot(a, b):
    return jnp.dot(a.astype(BF16), b.astype(BF16), preferred_element_type=F32)


def _dot_nt(a, b):
    return lax.dot_general(a.astype(BF16), b.astype(BF16), (((1,), (1,)), ((), ())),
                           preferred_element_type=F32)


def _dot_tn(a, b):
    return lax.dot_general(a.astype(BF16), b.astype(BF16), (((0,), (0,)), ((), ())),
                           preferred_element_type=F32)


def _split3(x):
    hi = x.astype(BF16)
    r1 = x - hi.astype(F32)
    mid = r1.astype(BF16)
    lo = (r1 - mid.astype(F32)).astype(BF16)
    return hi, mid, lo


def _silu(x):
    return x * jax.nn.sigmoid(x)


def _norm_mm_kernel(*refs, n_streams, n_out, epilogue, has_small, row_chunk):
    x_ref, g_ref = refs[0], refs[1]
    w_refs = refs[2:2 + n_streams]
    pos = 2 + n_streams
    ws_ref = refs[pos] if has_small else None
    pos += int(has_small)
    out_refs = refs[pos:pos + n_out]
    pos += n_out
    small_out_ref = refs[pos] if has_small else None
    h_ref = refs[-1]
    tm = x_ref.shape[0]

    @pl.when(pl.program_id(1) == 0)
    def _():
        for r in range(0, tm, row_chunk):
            xf = x_ref[r:r + row_chunk, :]
            ms = jnp.mean(xf * xf, axis=-1, keepdims=True)
            h_ref[r:r + row_chunk, :] = (xf * lax.rsqrt(ms + EPS) * g_ref[...]).astype(BF16)
        if has_small:
            small_out_ref[...] = jnp.dot(h_ref[...], ws_ref[...], preferred_element_type=F32)

    h = h_ref[...]
    accs = [jnp.dot(h, w[...], preferred_element_type=F32) for w in w_refs]
    outs = epilogue(*accs)
    for o_ref, o in zip(out_refs, outs):
        o_ref[...] = o.astype(o_ref.dtype)


def norm_matmul(x, gain, w, layer, *, name, n_streams, epilogue, n_out, out_dtype, tm, tn, w_small=None):
    t, k = x.shape
    n = w.shape[2] // n_streams
    tm, tn = _tile(t, tm), _tile(n, tn)
    nj = n // tn
    has_small = w_small is not None
    in_specs = [pl.BlockSpec((tm, k), lambda i, j: (i, 0)),
                pl.BlockSpec((1, k), lambda i, j: (0, 0))]
    args = [x, gain.reshape(1, k).astype(F32)]
    for s in range(n_streams):
        in_specs.append(pl.BlockSpec((pl.Squeezed(), k, tn), lambda i, j, s=s: (layer, 0, j + s * nj)))
        args.append(w)
    out_shape = [jax.ShapeDtypeStruct((t, n), out_dtype) for _ in range(n_out)]
    out_specs = [pl.BlockSpec((tm, tn), lambda i, j: (i, j)) for _ in range(n_out)]
    if has_small:
        ns = w_small.shape[2]
        in_specs.append(pl.BlockSpec((pl.Squeezed(), k, ns), lambda i, j: (layer, 0, 0)))
        args.append(w_small)
        out_shape.append(jax.ShapeDtypeStruct((t, ns), F32))
        out_specs.append(pl.BlockSpec((tm, ns), lambda i, j: (i, 0)))
    body = functools.partial(_norm_mm_kernel, n_streams=n_streams, n_out=n_out, epilogue=epilogue,
                             has_small=has_small, row_chunk=_tile(tm, 128))
    return pl.pallas_call(
        body,
        grid=(t // tm, nj),
        in_specs=in_specs,
        out_specs=out_specs,
        out_shape=out_shape,
        scratch_shapes=[pltpu.VMEM((tm, k), BF16)],
        compiler_params=_params("parallel", "arbitrary"),
        name=name,
    )(*args)


def _swiglu_epilogue(gate, up):
    return (_silu(gate) * up,)


def _identity_epilogue(acc):
    return (acc,)


def _shortconv_epilogue(gb, gc, xv):
    return gb, gc * xv


def _mm_residual_kernel(a_ref, w_ref, x_ref, o_ref, *, scale):
    acc = jnp.dot(a_ref[...], w_ref[...], preferred_element_type=F32)
    o_ref[...] = x_ref[...] + scale * acc


def matmul_residual(a, w, layer, x, *, name, scale, tm, tn):
    t, k = a.shape
    n = w.shape[2]
    tm, tn = _tile(t, tm), _tile(n, tn)
    return pl.pallas_call(
        functools.partial(_mm_residual_kernel, scale=scale),
        grid=(t // tm, n // tn),
        in_specs=[pl.BlockSpec((tm, k), lambda i, j: (i, 0)),
                  pl.BlockSpec((pl.Squeezed(), k, tn), lambda i, j: (layer, 0, j)),
                  pl.BlockSpec((tm, tn), lambda i, j: (i, j))],
        out_specs=pl.BlockSpec((tm, tn), lambda i, j: (i, j)),
        out_shape=jax.ShapeDtypeStruct((t, n), F32),
        compiler_params=_params("parallel", "parallel"),
        name=name,
    )(a, w, x)


def _causal_conv(buf_ref, src, w_ref, seq_step):
    ts = src.shape[0]
    width = w_ref.shape[0]

    @pl.when(seq_step == 0)
    def _():
        buf_ref[0:CONV_HALO, :] = jnp.zeros((CONV_HALO, buf_ref.shape[1]), F32)

    buf_ref[CONV_HALO:CONV_HALO + ts, :] = src
    acc = None
    for tap in range(width):
        start = CONV_HALO - (width - 1) + tap
        term = buf_ref[start:start + ts, :] * w_ref[tap:tap + 1, :]
        acc = term if acc is None else acc + term
    buf_ref[0:CONV_HALO, :] = buf_ref[ts:ts + CONV_HALO, :]
    return acc


def _shortconv_gate_kernel(gb_ref, u_ref, w_ref, o_ref, buf_ref):
    conv = _causal_conv(buf_ref, u_ref[0].astype(F32), w_ref, pl.program_id(2))
    o_ref[0] = (gb_ref[0].astype(F32) * conv).astype(o_ref.dtype)


def shortconv_gate(gb, u, conv_w, *, ts, tc):
    b, s, d = gb.shape
    ts, tc = _tile(s, ts), _tile(d, tc)
    assert conv_w.shape[0] - 1 <= CONV_HALO
    blk = pl.BlockSpec((1, ts, tc), lambda bi, ci, si: (bi, si, ci))
    return pl.pallas_call(
        _shortconv_gate_kernel,
        grid=(b, d // tc, s // ts),
        in_specs=[blk, blk, pl.BlockSpec((conv_w.shape[0], tc), lambda bi, ci, si: (0, ci))],
        out_specs=blk,
        out_shape=jax.ShapeDtypeStruct((b, s, d), BF16),
        scratch_shapes=[pltpu.VMEM((CONV_HALO + ts, tc), F32)],
        compiler_params=_params("parallel", "parallel", "arbitrary"),
        name="shortconv_gate",
    )(gb, u, conv_w)


def _gdn_gates_kernel(a_ref, bt_ref, alog_ref, dtb_ref, gc_ref, beta_ref):
    c = GDN_CHUNK
    xs = a_ref[...] + dtb_ref[...]
    softplus = jnp.maximum(xs, 0.0) + jnp.log(1.0 + jnp.exp(-jnp.abs(xs)))
    g = -jnp.exp(alog_ref[...]) * softplus
    row = lax.broadcasted_iota(jnp.int32, (c, c), 0)
    col = lax.broadcasted_iota(jnp.int32, (c, c), 1)
    tri = (row >= col).astype(BF16)
    for r in range(0, g.shape[0], c):
        hi, mid, lo = _split3(g[r:r + c, :])
        gc_ref[r:r + c, :] = (jnp.dot(tri, hi, preferred_element_type=F32)
                              + jnp.dot(tri, mid, preferred_element_type=F32)
                              + jnp.dot(tri, lo, preferred_element_type=F32))
    beta_ref[...] = jax.nn.sigmoid(bt_ref[...])


def gdn_gates(ab, a_log, dt_bias, *, tb):
    t = ab.shape[0]
    h = a_log.shape[0]
    tb = _tile(t, tb)
    pad = lambda v: jnp.pad(v.astype(F32), (0, LANES - h)).reshape(1, LANES)
    row_blk = lambda j: pl.BlockSpec((tb, LANES), lambda i, j=j: (i, j))
    vec = pl.BlockSpec((1, LANES), lambda i: (0, 0))
    return pl.pallas_call(
        _gdn_gates_kernel,
        grid=(t // tb,),
        in_specs=[row_blk(0), row_blk(1), vec, vec],
        out_specs=[row_blk(0), row_blk(0)],
        out_shape=[jax.ShapeDtypeStruct((t, LANES), F32)] * 2,
        compiler_params=_params("parallel"),
        name="gdn_gates",
    )(ab, ab, pad(a_log), pad(dt_bias))


def _gdn_chunk_kernel(q_ref, k_ref, v_ref, z_ref, wq_ref, wk_ref, wv_ref, gcc_ref, bc_ref, gcr_ref,
                      onorm_ref, o_ref, bq_ref, bk_ref, bv_ref, state_ref, *, heads_per_step):
    c = GDN_CHUNK
    dk = HEAD_DIM
    seq_step = pl.program_id(2)
    ts = q_ref.shape[1]

    @pl.when(seq_step == 0)
    def _():
        state_ref[...] = jnp.zeros(state_ref.shape, F32)

    yq = _silu(_causal_conv(bq_ref, q_ref[0].astype(F32), wq_ref, seq_step))
    yk = _silu(_causal_conv(bk_ref, k_ref[0].astype(F32), wk_ref, seq_step))
    yv = _silu(_causal_conv(bv_ref, v_ref[0].astype(F32), wv_ref, seq_step))

    row = lax.broadcasted_iota(jnp.int32, (c, c), 0)
    col = lax.broadcasted_iota(jnp.int32, (c, c), 1)
    incl = row >= col
    strict = row > col
    eye = (row == col).astype(F32)
    lane = lax.broadcasted_iota(jnp.int32, (ts, LANES), 1)
    hb, nc = heads_per_step, ts // c

    per_head = []
    for i in range(hb):
        head = pl.program_id(1) * hb + i
        hl = slice(i * dk, (i + 1) * dk)
        qh, kh, vh = yq[:, hl], yk[:, hl], yv[:, hl]
        qh = qh * lax.rsqrt(jnp.sum(qh * qh, axis=-1, keepdims=True) + EPS) * (dk ** -0.5)
        kh = kh * lax.rsqrt(jnp.sum(kh * kh, axis=-1, keepdims=True) + EPS)
        pick = lane == head
        gcol = jnp.sum(jnp.where(pick, gcc_ref[0], 0.0), axis=-1, keepdims=True)
        bcol = jnp.sum(jnp.where(pick, bc_ref[0], 0.0), axis=-1, keepdims=True)
        grow = gcr_ref[0, pl.ds(head, 1), :]
        per_head.append((qh, kh, vh, gcol, bcol, grow))
    items = [(ci, i) for ci in range(nc) for i in range(hb)]
    qs, ks, vs, gcols, bcols, grows = [], [], [], [], [], []
    for ci, i in items:
        qh, kh, vh, gcol, bcol, grow = per_head[i]
        sl = slice(ci * c, (ci + 1) * c)
        qs.append(qh[sl])
        ks.append(kh[sl])
        vs.append(vh[sl])
        gcols.append(gcol[sl])
        bcols.append(bcol[sl])
        grows.append(grow[:, sl])
    n = len(items)
    rng = range(n)
    gammas = [jnp.where(incl, jnp.exp(jnp.where(incl, gcols[t] - grows[t], 0.0)), 0.0) for t in rng]
    kbs = [ks[t] * bcols[t] for t in rng]
    n_pows = [jnp.where(strict, -(_dot_nt(kbs[t], ks[t]) * gammas[t]), 0.0) for t in rng]
    t_invs = [eye + n_pows[t] for t in rng]
    for _ in range(5):
        n_pows = [_dot(n_pows[t], n_pows[t]) for t in rng]
        t_invs = [t_invs[t] + _dot(t_invs[t], n_pows[t]) for t in rng]
    e_gs = [jnp.exp(gcols[t]) for t in rng]
    wus = [_dot(t_invs[t], jnp.concatenate([kbs[t] * e_gs[t], vs[t] * bcols[t]], axis=1)) for t in rng]
    qks = [jnp.where(incl, _dot_nt(qs[t], ks[t]) * gammas[t], 0.0) for t in rng]
    g_lasts = [grows[t][:, c - 1:c] for t in rng]
    q_decs = [qs[t] * e_gs[t] for t in rng]
    k_decs = [ks[t] * jnp.exp(g_lasts[t] - gcols[t]) for t in rng]
    decays = [jnp.exp(g_lasts[t]) for t in rng]

    states = [state_ref[i] for i in range(hb)]
    outs = [None] * n
    for ci in range(nc):
        idx = [ci * hb + i for i in range(hb)]
        v_news = [wus[t][:, dk:] - _dot(wus[t][:, :dk], states[i]) for i, t in enumerate(idx)]
        o_state = [_dot(q_decs[t], states[i]) for i, t in enumerate(idx)]
        for i, t in enumerate(idx):
            outs[t] = o_state[i] + _dot(qks[t], v_news[i])
        states = [states[i] * decays[t] + _dot_tn(k_decs[t], v_news[i]) for i, t in enumerate(idx)]
    for i in range(hb):
        state_ref[i] = states[i]

    for t, (ci, i) in enumerate(items):
        o = outs[t]
        o = o * lax.rsqrt(jnp.mean(o * o, axis=-1, keepdims=True) + EPS) * onorm_ref[...]
        z = z_ref[0, ci * c:(ci + 1) * c, i * dk:(i + 1) * dk].astype(F32)
        o_ref[0, ci * c:(ci + 1) * c, i * dk:(i + 1) * dk] = (o * _silu(z)).astype(o_ref.dtype)


def gdn_chunk(proj, conv_w, gcc, betac, gcr, o_norm, *, n_heads, ts, heads_per_step):
    b, s, _ = proj.shape
    dk = HEAD_DIM
    ts = _tile(s, ts)
    hb = _tile(n_heads, heads_per_step)
    groups = n_heads // hb
    assert ts % GDN_CHUNK == 0 and conv_w.shape[0] - 1 <= CONV_HALO
    width = conv_w.shape[0]
    tok = lambda part: pl.BlockSpec((1, ts, hb * dk), lambda bi, hi, si, p=part: (bi, si, hi + p * groups))
    cw = lambda part: pl.BlockSpec((width, hb * dk), lambda bi, hi, si, p=part: (0, hi + p * groups))
    gate_col = pl.BlockSpec((1, ts, LANES), lambda bi, hi, si: (bi, si, 0))
    return pl.pallas_call(
        functools.partial(_gdn_chunk_kernel, heads_per_step=hb),
        grid=(b, groups, s // ts),
        in_specs=[tok(0), tok(1), tok(2), tok(3), cw(0), cw(1), cw(2), gate_col, gate_col,
                  pl.BlockSpec((1, n_heads, ts), lambda bi, hi, si: (bi, 0, si)),
                  pl.BlockSpec((1, dk), lambda bi, hi, si: (0, 0))],
        out_specs=pl.BlockSpec((1, ts, hb * dk), lambda bi, hi, si: (bi, si, hi)),
        out_shape=jax.ShapeDtypeStruct((b, s, n_heads * dk), BF16),
        scratch_shapes=[pltpu.VMEM((CONV_HALO + ts, hb * dk), F32)] * 3
                       + [pltpu.VMEM((hb, dk, dk), F32)],
        compiler_params=_params("parallel", "parallel", "arbitrary"),
        name="gdn_chunk",
    )(proj, proj, proj, proj, conv_w, conv_w, conv_w, gcc, betac, gcr,
      o_norm.reshape(1, dk).astype(F32))


def gdn_split_weights(w_in, n_heads):
    wide = 4 * n_heads * HEAD_DIM
    pad = ((0, 0), (0, 0), (0, LANES - n_heads))
    w_a = jnp.pad(w_in[:, :, wide:wide + n_heads], pad)
    w_b = jnp.pad(w_in[:, :, wide + n_heads:], pad)
    return w_in[:, :, :wide].astype(BF16), jnp.concatenate([w_a, w_b], axis=2).astype(BF16)


def gated_deltanet(x, gain, w_big, w_small, layer, conv_w, a_log, dt_bias, o_norm, w_out, batch):
    t, d = x.shape
    n_heads = a_log.shape[0]
    s = t // batch
    wide = 4 * n_heads * HEAD_DIM
    proj, ab = norm_matmul(x, gain, w_big, layer, name="gdn_in_proj", n_streams=1,
                           epilogue=_identity_epilogue, n_out=1, out_dtype=BF16, tm=512, tn=1024,
                           w_small=w_small)
    gcc, betac = gdn_gates(ab, a_log, dt_bias, tb=512)
    gcc = gcc.reshape(batch, s, LANES)
    betac = betac.reshape(batch, s, LANES)
    gcr = jnp.swapaxes(gcc[:, :, :n_heads], 1, 2)
    o = gdn_chunk(proj.reshape(batch, s, wide), conv_w.astype(F32), gcc, betac, gcr, o_norm,
                  n_heads=n_heads, ts=256, heads_per_step=4)
    return matmul_residual(o.reshape(t, n_heads * HEAD_DIM), w_out, layer, x, name="gdn_out_proj",
                           scale=1.0, tm=512, tn=1024)


def short_conv(x, gain, w_in, layer, conv_w, w_out, batch):
    t, d = x.shape
    gb, u = norm_matmul(x, gain, w_in, layer, name="sc_in_proj", n_streams=3,
                        epilogue=_shortconv_epilogue, n_out=2, out_dtype=BF16, tm=512, tn=256)
    s = t // batch
    y = shortconv_gate(gb.reshape(batch, s, d), u.reshape(batch, s, d), conv_w.astype(F32),
                       ts=512, tc=512)
    return matmul_residual(y.reshape(t, d), w_out, layer, x, name="sc_out_proj", scale=1.0,
                           tm=512, tn=1024)


def _rope_table_kernel(pos_ref, freq_ref, cos_ref, sin_ref):
    ang = pos_ref[0].astype(F32) * freq_ref[...]
    lane = lax.broadcasted_iota(jnp.int32, ang.shape, 1)
    sign = jnp.where(lane < HEAD_DIM // 2, -1.0, 1.0)
    cos_ref[0] = jnp.cos(ang)
    sin_ref[0] = jnp.sin(ang) * sign


def rope_tables(positions, *, ts):
    b, s = positions.shape
    ts = _tile(s, ts)
    half = HEAD_DIM // 2
    inv_freq = ROPE_THETA ** (-jnp.arange(half, dtype=F32) / half)
    freq = jnp.concatenate([inv_freq, inv_freq]).reshape(1, HEAD_DIM)
    blk = pl.BlockSpec((1, ts, HEAD_DIM), lambda bi, si: (bi, si, 0))
    return pl.pallas_call(
        _rope_table_kernel,
        grid=(b, s // ts),
        in_specs=[pl.BlockSpec((1, ts, 1), lambda bi, si: (bi, si, 0)),
                  pl.BlockSpec((1, HEAD_DIM), lambda bi, si: (0, 0))],
        out_specs=[blk, blk],
        out_shape=[jax.ShapeDtypeStruct((b, s, HEAD_DIM), F32)] * 2,
        compiler_params=_params("parallel", "parallel"),
        name="rope_tables",
    )(positions.reshape(b, s, 1), freq)


def _moba_prep_kernel(q_ref, k_ref, v_ref, cos_ref, sin_ref, qn_ref, kn_ref,
                      qo_ref, ko_ref, vto_ref, km_ref, *, heads_per_step):
    dh = HEAD_DIM
    cos, sin = cos_ref[0], sin_ref[0]

    def norm_rope(xh, gain):
        xh = xh * lax.rsqrt(jnp.mean(xh * xh, axis=-1, keepdims=True) + EPS) * gain
        return xh * cos + pltpu.roll(xh, dh // 2, axis=1) * sin

    for i in range(heads_per_step):
        hl = slice(i * dh, (i + 1) * dh)
        qh = norm_rope(q_ref[0, :, hl].astype(F32), qn_ref[...])
        kh = norm_rope(k_ref[0, :, hl].astype(F32), kn_ref[...])
        qo_ref[0, :, hl] = (qh * (dh ** -0.5 * LOG2_E)).astype(qo_ref.dtype)
        ko_ref[0, i, 0] = kh.astype(ko_ref.dtype)
        km_ref[0, i, 0] = jnp.mean(kh, axis=0, keepdims=True)
        vto_ref[0, i, 0, 0:dh, :] = v_ref[0, :, hl].astype(F32).T.astype(vto_ref.dtype)
        sub = lax.broadcasted_iota(jnp.int32, (SUM_ROWS, MOBA_BLOCK), 0)
        vto_ref[0, i, 0, dh:dh + SUM_ROWS, :] = jnp.where(sub == 0, 1.0, 0.0).astype(vto_ref.dtype)


def moba_prep(qkv, cos, sin, q_norm, k_norm, *, n_heads, heads_per_step):
    b, s, _ = qkv.shape
    dh, bs = HEAD_DIM, MOBA_BLOCK
    assert s % bs == 0
    nb = s // bs
    hb = _tile(n_heads, heads_per_step)
    groups = n_heads // hb
    tok = lambda part: pl.BlockSpec((1, bs, hb * dh), lambda bi, si, hi, p=part: (bi, si, hi + p * groups))
    tab = pl.BlockSpec((1, bs, dh), lambda bi, si, hi: (bi, si, 0))
    vec = pl.BlockSpec((1, dh), lambda bi, si, hi: (0, 0))
    return pl.pallas_call(
        functools.partial(_moba_prep_kernel, heads_per_step=hb),
        grid=(b, nb, groups),
        in_specs=[tok(0), tok(1), tok(2), tab, tab, vec, vec],
        out_specs=[pl.BlockSpec((1, bs, hb * dh), lambda bi, si, hi: (bi, si, hi)),
                   pl.BlockSpec((1, hb, 1, bs, dh), lambda bi, si, hi: (bi, hi, si, 0, 0)),
                   pl.BlockSpec((1, hb, 1, dh + SUM_ROWS, bs), lambda bi, si, hi: (bi, hi, si, 0, 0)),
                   pl.BlockSpec((1, hb, 1, 1, dh), lambda bi, si, hi: (bi, hi, si, 0, 0))],
        out_shape=[jax.ShapeDtypeStruct((b, s, n_heads * dh), BF16),
                   jax.ShapeDtypeStruct((b, n_heads, nb, bs, dh), BF16),
                   jax.ShapeDtypeStruct((b, n_heads, nb, dh + SUM_ROWS, bs), BF16),
                   jax.ShapeDtypeStruct((b, n_heads, nb, 1, dh), F32)],
        compiler_params=_params("parallel", "parallel", "parallel"),
        name="moba_prep",
    )(qkv, qkv, qkv, cos, sin, q_norm.reshape(1, dh).astype(F32), k_norm.reshape(1, dh).astype(F32))


def _moba_attn_kernel(q_ref, k_ref, vt_ref, km_ref, o_ref, bias_ref, *, group, slab):
    bs, dh = MOBA_BLOCK, HEAD_DIM
    nb = km_ref.shape[2]
    qi = pl.program_id(2)
    q = q_ref[0]

    gate = None
    for part in _split3(km_ref[0, 0]):
        term = _dot_nt(part, q)
        gate = term if gate is None else gate + term
    blk = lax.broadcasted_iota(jnp.int32, (nb, bs), 0)
    past = blk < qi
    gate = jnp.where(past, gate, -jnp.inf)
    blk_f = blk.astype(F32)
    bias = jnp.full((nb, bs), -jnp.inf, F32)
    for _ in range(MOBA_TOPK):
        top = jnp.max(gate, axis=0, keepdims=True)
        first = jnp.min(jnp.where(gate == top, blk_f, float(nb)), axis=0, keepdims=True)
        hit = (blk_f == first) & (top > -jnp.inf)
        bias = jnp.where(hit, 0.0, bias)
        gate = jnp.where(hit, -jnp.inf, gate)
    bias_ref[...] = bias

    def scores(j):
        return _dot_nt(k_ref[0, 0, j], q)

    def slab_scores(j):
        return _dot_nt(k_ref[0, 0, pl.ds(j, slab)].reshape(slab * bs, dh), q)

    def pieces(s_t):
        m_blk = jnp.max(s_t, axis=0, keepdims=True)
        return m_blk, jnp.exp2(s_t - m_blk).astype(BF16)

    def values(j, p):
        return jnp.dot(vt_ref[0, 0, j], p, preferred_element_type=F32)[:dh + 8]

    def merge(m, acc, parts):
        m_new = functools.reduce(jnp.maximum, [m] + [mb for mb, _ in parts])
        acc = jnp.exp2(m - m_new) * acc
        for mb, ab in parts:
            acc = acc + jnp.exp2(mb - m_new) * ab
        return m_new, acc

    key_pos = lax.broadcasted_iota(jnp.int32, (bs, bs), 0)
    qry_pos = lax.broadcasted_iota(jnp.int32, (bs, bs), 1)
    m, p = pieces(jnp.where(key_pos <= qry_pos, scores(qi), -jnp.inf))
    acc = values(qi, p)

    def body(it, carry):
        m, acc = carry
        j0 = it * group
        slabs = [slab_scores(j0 + u * slab) for u in range(group // slab)]
        mp = [pieces(slabs[b // slab][(b % slab) * bs:(b % slab + 1) * bs]) for b in range(group)]
        parts = [(mp[b][0] + bias_ref[pl.ds(j0 + b, 1), :], values(j0 + b, mp[b][1]))
                 for b in range(group)]
        return merge(m, acc, parts)

    m, acc = lax.fori_loop(0, (qi + group - 1) // group, body, (m, acc))
    o_ref[0] = (acc[:dh] / acc[dh:dh + 1]).T.astype(o_ref.dtype)


def moba_attn(q, k, vt, kmean):
    b, s, d = q.shape
    _, n_heads, nb, bs, dh = k.shape
    group = _tile(nb, 8)
    slab = _tile(group, 2)
    return pl.pallas_call(
        functools.partial(_moba_attn_kernel, group=group, slab=slab),
        grid=(b, n_heads, nb),
        in_specs=[pl.BlockSpec((1, bs, dh), lambda bi, hi, qi: (bi, qi, hi)),
                  pl.BlockSpec((1, 1, nb, bs, dh), lambda bi, hi, qi: (bi, hi, 0, 0, 0)),
                  pl.BlockSpec((1, 1, nb, dh + SUM_ROWS, bs), lambda bi, hi, qi: (bi, hi, 0, 0, 0)),
                  pl.BlockSpec((1, 1, nb, dh), lambda bi, hi, qi: (bi, hi, 0, 0))],
        out_specs=pl.BlockSpec((1, bs, dh), lambda bi, hi, qi: (bi, qi, hi)),
        out_shape=jax.ShapeDtypeStruct((b, s, d), BF16),
        scratch_shapes=[pltpu.VMEM((nb, bs), F32)],
        compiler_params=_params("parallel", "parallel", "arbitrary"),
        name="moba_attn",
    )(q, k, vt, kmean)


def moba_attention(x, gain, positions, w_in, layer, q_norm, k_norm, w_out):
    t, d = x.shape
    batch, s = positions.shape
    n_heads = d // HEAD_DIM
    (qkv,) = norm_matmul(x, gain, w_in, layer, name="moba_in_proj", n_streams=1,
                         epilogue=_identity_epilogue, n_out=1, out_dtype=BF16, tm=512, tn=1024)
    cos, sin = rope_tables(positions, ts=512)
    q, k, vt, kmean = moba_prep(qkv.reshape(batch, s, 3 * d), cos, sin, q_norm, k_norm,
                                n_heads=n_heads, heads_per_step=4)
    nb = s // MOBA_BLOCK
    o = moba_attn(q, k, vt, kmean.reshape(batch, n_heads, nb, HEAD_DIM))
    return matmul_residual(o.reshape(t, d), w_out, layer, x, name="moba_out_proj", scale=1.0,
                           tm=512, tn=1024)


def swiglu_half_step(x, gain, w_in, w_out, layer):
    (act,) = norm_matmul(x, gain, w_in, layer, name="ffn_in", n_streams=2, epilogue=_swiglu_epilogue,
                         n_out=1, out_dtype=BF16, tm=512, tn=512)
    return matmul_residual(act, w_out, layer, x, name="ffn_out", scale=0.5, tm=512, tn=1024)


def kernel(x, positions, norm_ffn1, norm_mix, norm_ffn2, ffn1_w_in, ffn1_w_out, ffn2_w_in, ffn2_w_out, gdn_w_in, gdn_conv_w, gdn_a_log, gdn_dt_bias, gdn_out_norm, gdn_w_out, sc_w_in, sc_conv_w, sc_w_out, moba_w_in, moba_q_norm, moba_k_norm, moba_w_out):
    batch, s, d = x.shape
    depth = norm_ffn1.shape[0]
    ffn1_w_in, ffn1_w_out, ffn2_w_in, ffn2_w_out, gdn_w_out, sc_w_in, sc_w_out, moba_w_in, moba_w_out = (
        w.astype(BF16) for w in (ffn1_w_in, ffn1_w_out, ffn2_w_in, ffn2_w_out, gdn_w_out, sc_w_in,
                                 sc_w_out, moba_w_in, moba_w_out))
    gdn_w_big, gdn_w_small = gdn_split_weights(gdn_w_in, gdn_a_log.shape[1])
    xt = x.reshape(batch * s, d)
    for i in range(depth):
        kind, j = i % 3, i // 3
        xt = swiglu_half_step(xt, norm_ffn1[i], ffn1_w_in, ffn1_w_out, i)
        if kind == 0:
            xt = gated_deltanet(xt, norm_mix[i], gdn_w_big, gdn_w_small, j, gdn_conv_w[j], gdn_a_log[j],
                                gdn_dt_bias[j], gdn_out_norm[j], gdn_w_out, batch)
        elif kind == 1:
            xt = short_conv(xt, norm_mix[i], sc_w_in, j, sc_conv_w[j], sc_w_out, batch)
        else:
            xt = moba_attention(xt, norm_mix[i], positions, moba_w_in, j, moba_q_norm[j],
                                moba_k_norm[j], moba_w_out)
        xt = swiglu_half_step(xt, norm_ffn2[i], ffn2_w_in, ffn2_w_out, i)
    return xt.reshape(batch, s, d)
```

```python
import functools

import jax
import jax.numpy as jnp
from jax import lax
from jax.experimental import pallas as pl
from jax.experimental.pallas import tpu as pltpu

F32 = jnp.float32
BF16 = jnp.bfloat16

EPS = 1e-6
HEAD_DIM = 128
GDN_CHUNK = 64
MOBA_BLOCK = 256
MOBA_TOPK = 3
ROPE_THETA = 10000.0
LOG2_E = 1.4426950408889634
SUM_ROWS = 16

LANES = 128
MXU_COLS = 256
HALO_ROWS = 16
VMEM_LIMIT = 56 * 1024 * 1024
ROW_TILE = 1024
RES_ROW_TILE = 512
RES_COL_TILE = 1024


def _tile(dim, want):
    t = min(dim, want)
    while dim % t:
        t //= 2
    return t


def _params(*sem):
    return pltpu.CompilerParams(dimension_semantics=sem, vmem_limit_bytes=VMEM_LIMIT)


def _dot(a, b):
    return jnp.dot(a.astype(BF16), b.astype(BF16), preferred_element_type=F32)


def _dot_nt(a, b):
    return lax.dot_general(a.astype(BF16), b.astype(BF16), (((1,), (1,)), ((), ())),
                           preferred_element_type=F32)


def _dot_tn(a, b):
    return lax.dot_general(a.astype(BF16), b.astype(BF16), (((0,), (0,)), ((), ())),
                           preferred_element_type=F32)


def _split3(x):
    hi = x.astype(BF16)
    r1 = x - hi.astype(F32)
    mid = r1.astype(BF16)
    lo = (r1 - mid.astype(F32)).astype(BF16)
    return hi, mid, lo


def _silu(x):
    return x * jax.nn.sigmoid(x)


def _fold_lanes(x):
    parts = [x[:, c * LANES:(c + 1) * LANES] for c in range(x.shape[1] // LANES)]
    return functools.reduce(jnp.add, parts)


def _scale_rows_kernel(x_ref, g_ref, xb_ref, ss_ref):
    x = x_ref[...]
    xb_ref[...] = (x * g_ref[...]).astype(BF16)
    ss_ref[...] = _fold_lanes(x * x)


def scale_rows(x, gain, *, tm):
    t, d = x.shape
    tm = _tile(t, tm)
    return pl.pallas_call(
        _scale_rows_kernel,
        grid=(t // tm,),
        in_specs=[pl.BlockSpec((tm, d), lambda i: (i, 0)), pl.BlockSpec((1, d), lambda i: (0, 0))],
        out_specs=[pl.BlockSpec((tm, d), lambda i: (i, 0)), pl.BlockSpec((tm, LANES), lambda i: (i, 0))],
        out_shape=[jax.ShapeDtypeStruct((t, d), BF16), jax.ShapeDtypeStruct((t, LANES), F32)],
        compiler_params=_params("parallel"),
        name="scale_rows",
    )(x, gain.reshape(1, d).astype(F32))


def _mm_residual_kernel(*refs, scale, emit_next):
    if emit_next:
        a_ref, w_ref, x_ref, g_ref, o_ref, xb_ref, ss_ref = refs
    else:
        a_ref, w_ref, x_ref, o_ref = refs
    xn = x_ref[...] + scale * jnp.dot(a_ref[...], w_ref[...], preferred_element_type=F32)
    o_ref[...] = xn
    if emit_next:
        xb_ref[...] = (xn * g_ref[...]).astype(BF16)
        part = _fold_lanes(xn * xn)

        @pl.when(pl.program_id(1) == 0)
        def _():
            ss_ref[...] = part

        @pl.when(pl.program_id(1) != 0)
        def _():
            ss_ref[...] += part


def matmul_residual(a, w, layer, x, next_gain, *, name, scale):
    t, k = a.shape
    n = w.shape[2]
    tm, tn = _tile(t, RES_ROW_TILE), _tile(n, RES_COL_TILE)
    emit_next = next_gain is not None
    tile = pl.BlockSpec((tm, tn), lambda i, j: (i, j))
    in_specs = [pl.BlockSpec((tm, k), lambda i, j: (i, 0)),
                pl.BlockSpec((pl.Squeezed(), k, tn), lambda i, j: (layer, 0, j)),
                tile]
    args = [a, w, x]
    out_specs, out_shape = [tile], [jax.ShapeDtypeStruct((t, n), F32)]
    if emit_next:
        in_specs.append(pl.BlockSpec((1, tn), lambda i, j: (0, j)))
        args.append(next_gain.reshape(1, n).astype(F32))
        out_specs += [tile, pl.BlockSpec((tm, LANES), lambda i, j: (i, 0))]
        out_shape += [jax.ShapeDtypeStruct((t, n), BF16), jax.ShapeDtypeStruct((t, LANES), F32)]
    outs = pl.pallas_call(
        functools.partial(_mm_residual_kernel, scale=scale, emit_next=emit_next),
        grid=(t // tm, n // tn),
        in_specs=in_specs,
        out_specs=out_specs,
        out_shape=out_shape,
        compiler_params=_params("parallel", "arbitrary"),
        name=name,
    )(*args)
    return tuple(outs) if emit_next else (outs[0], None, None)


def _proj_kernel(*refs, n_streams, n_extra, n_out, epilogue, halo, seq_len, k_dim):
    xb_ref, ss_ref = refs[0], refs[1]
    pos = 2
    if halo:
        xh_ref, sh_ref = refs[2], refs[3]
        pos = 4
    w_refs = refs[pos:pos + n_streams]
    pos += n_streams
    extra_refs = refs[pos:pos + n_extra]
    pos += n_extra
    out_refs = refs[pos:pos + n_out]
    pos += n_out
    scratch = refs[pos:]
    r_ref = scratch[0]
    tm = xb_ref.shape[0]
    hr = HALO_ROWS if halo else 0

    def row_factor(ss):
        return lax.rsqrt(jnp.sum(ss, axis=-1, keepdims=True) * (1.0 / k_dim) + EPS)

    seq_start = (pl.program_id(0) * tm) % seq_len == 0 if halo else None

    @pl.when(pl.program_id(1) == 0)
    def _():
        r_ref[hr:hr + tm, :] = row_factor(ss_ref[...])
        if halo:
            a_ref = scratch[1]
            a_ref[hr:hr + tm, :] = xb_ref[...]
            a_ref[0:hr, :] = xh_ref[...]
            r_ref[0:hr, :] = jnp.where(seq_start, 0.0, row_factor(sh_ref[...]))

    a = scratch[1][...] if halo else xb_ref[...]
    r = r_ref[...]
    tn = w_refs[0].shape[1]
    strip = min(tn, MXU_COLS)
    for c0 in range(0, tn, strip):
        cols = slice(c0, c0 + strip)
        accs = [jnp.dot(a, w[:, cols], preferred_element_type=F32) * r for w in w_refs]
        epilogue(accs, extra_refs, out_refs, cols)


def projection(xb, ss, w, layer, *, name, col0, n_cols, n_streams, epilogue, outs, tn,
               extras=(), halo=False, seq_len=None):
    t, k = xb.shape
    tm, tn = _tile(t, ROW_TILE), _tile(n_cols, tn)
    assert col0 % tn == 0
    nj, j0 = n_cols // tn, col0 // tn
    rows_mode = pl.Buffered(1) if (halo or tn >= 1024) else None
    in_specs = [pl.BlockSpec((tm, k), lambda i, j: (i, 0), pipeline_mode=rows_mode),
                pl.BlockSpec((tm, LANES), lambda i, j: (i, 0))]
    args = [xb, ss]
    scratch = [pltpu.VMEM((tm + (HALO_ROWS if halo else 0), 1), F32)]
    if halo:
        assert seq_len % tm == 0 and tm % HALO_ROWS == 0
        per = tm // HALO_ROWS
        prev = lambda i, j: (jnp.maximum(i * per - 1, 0), 0)
        in_specs += [pl.BlockSpec((HALO_ROWS, k), prev), pl.BlockSpec((HALO_ROWS, LANES), prev)]
        args += [xb, ss]
        scratch.append(pltpu.VMEM((tm + HALO_ROWS, k), BF16))
    for s in range(n_streams):
        in_specs.append(pl.BlockSpec((pl.Squeezed(), k, tn),
                                     lambda i, j, s=s: (layer, 0, j0 + j + s * nj)))
        args.append(w)
    for arr, spec in extras:
        in_specs.append(spec)
        args.append(arr)
    body = functools.partial(_proj_kernel, n_streams=n_streams, n_extra=len(extras), n_out=len(outs),
                             epilogue=epilogue, halo=halo, seq_len=seq_len, k_dim=k)
    return pl.pallas_call(
        body,
        grid=(t // tm, nj),
        in_specs=in_specs,
        out_specs=[spec for _, spec in outs],
        out_shape=[shape for shape, _ in outs],
        scratch_shapes=scratch,
        compiler_params=_params("parallel", "arbitrary"),
        name=name,
    )(*args)


def _row_col_tile(t, n, dtype, tn):
    tm, tn = _tile(t, ROW_TILE), _tile(n, tn)
    return jax.ShapeDtypeStruct((t, n), dtype), pl.BlockSpec((tm, tn), lambda i, j: (i, j))


def _col_vec(v, tn):
    return v, pl.BlockSpec((v.shape[0], tn), lambda i, j: (0, j))


def _conv_rows(ext, w_ref, cols):
    width = w_ref.shape[0]
    tm = ext.shape[0] - HALO_ROWS
    acc = None
    for tap in range(width):
        start = HALO_ROWS - (width - 1) + tap
        term = ext[start:start + tm, :] * w_ref[tap:tap + 1, cols]
        acc = term if acc is None else acc + term
    return acc


def _heads(cols):
    return [(c // HEAD_DIM, slice(c, c + HEAD_DIM)) for c in range(cols.start, cols.stop, HEAD_DIM)]


def _swiglu_epilogue(accs, extra_refs, out_refs, cols):
    gate, up = accs
    out_refs[0][:, cols] = (_silu(gate) * up).astype(out_refs[0].dtype)


def _plain_epilogue(accs, extra_refs, out_refs, cols):
    out_refs[0][:, cols] = accs[0].astype(out_refs[0].dtype)


def _gdn_qk_epilogue(accs, extra_refs, out_refs, cols):
    conv_ref, scale_ref = extra_refs
    y = _silu(_conv_rows(accs[0], conv_ref, cols))
    for _, hl in _heads(cols):
        seg = y[:, hl.start - cols.start:hl.stop - cols.start]
        seg = seg * lax.rsqrt(jnp.sum(seg * seg, axis=-1, keepdims=True) + EPS) * scale_ref[:, hl]
        out_refs[0][:, hl] = seg.astype(out_refs[0].dtype)


def _gdn_v_epilogue(accs, extra_refs, out_refs, cols):
    out_refs[0][:, cols] = _silu(_conv_rows(accs[0], extra_refs[0], cols)).astype(out_refs[0].dtype)


def _gdn_gates_kernel(a_ref, bt_ref, alog_ref, dtb_ref, gc_ref, beta_ref):
    c = GDN_CHUNK
    xs = a_ref[...] + dtb_ref[...]
    softplus = jnp.maximum(xs, 0.0) + jnp.log(1.0 + jnp.exp(-jnp.abs(xs)))
    g = -jnp.exp(alog_ref[...]) * softplus
    row = lax.broadcasted_iota(jnp.int32, (c, c), 0)
    col = lax.broadcasted_iota(jnp.int32, (c, c), 1)
    tri = (row >= col).astype(BF16)
    for r in range(0, g.shape[0], c):
        hi, mid, lo = _split3(g[r:r + c, :])
        gc_ref[r:r + c, :] = (jnp.dot(tri, hi, preferred_element_type=F32)
                              + jnp.dot(tri, mid, preferred_element_type=F32)
                              + jnp.dot(tri, lo, preferred_element_type=F32))
    beta_ref[...] = jax.nn.sigmoid(bt_ref[...])


def gdn_gates(ab, a_log, dt_bias, *, tb):
    t = ab.shape[0]
    h = a_log.shape[0]
    tb = _tile(t, tb)
    pad = lambda v: jnp.pad(v.astype(F32), (0, LANES - h)).reshape(1, LANES)
    row_blk = lambda j: pl.BlockSpec((tb, LANES), lambda i, j=j: (i, j))
    vec = pl.BlockSpec((1, LANES), lambda i: (0, 0))
    return pl.pallas_call(
        _gdn_gates_kernel,
        grid=(t // tb,),
        in_specs=[row_blk(0), row_blk(1), vec, vec],
        out_specs=[row_blk(0), row_blk(0)],
        out_shape=[jax.ShapeDtypeStruct((t, LANES), F32)] * 2,
        compiler_params=_params("parallel"),
        name="gdn_gates",
    )(ab, ab, pad(a_log), pad(dt_bias))


def _gdn_chunk_kernel(q_ref, k_ref, v_ref, z_ref, gcc_ref, bc_ref, gcr_ref, onorm_ref, o_ref,
                      state_ref, *, heads_per_step):
    c = GDN_CHUNK
    dk = HEAD_DIM
    ts = q_ref.shape[1]

    @pl.when(pl.program_id(2) == 0)
    def _():
        state_ref[...] = jnp.zeros(state_ref.shape, F32)

    row = lax.broadcasted_iota(jnp.int32, (c, c), 0)
    col = lax.broadcasted_iota(jnp.int32, (c, c), 1)
    incl = row >= col
    strict = row > col
    eye = (row == col).astype(F32)
    lane = lax.broadcasted_iota(jnp.int32, (ts, LANES), 1)
    hb, nc = heads_per_step, ts // c

    per_head = []
    for i in range(hb):
        head = pl.program_id(1) * hb + i
        pick = lane == head
        gcol = jnp.sum(jnp.where(pick, gcc_ref[0], 0.0), axis=-1, keepdims=True)
        bcol = jnp.sum(jnp.where(pick, bc_ref[0], 0.0), axis=-1, keepdims=True)
        grow = gcr_ref[0, pl.ds(head, 1), :]
        per_head.append((gcol, bcol, grow))
    items = [(ci, i) for ci in range(nc) for i in range(hb)]
    qs, ks, vs, gcols, bcols, grows = [], [], [], [], [], []
    for ci, i in items:
        gcol, bcol, grow = per_head[i]
        sl = slice(ci * c, (ci + 1) * c)
        hl = slice(i * dk, (i + 1) * dk)
        qs.append(q_ref[0, sl, hl])
        ks.append(k_ref[0, sl, hl])
        vs.append(v_ref[0, sl, hl].astype(F32))
        gcols.append(gcol[sl])
        bcols.append(bcol[sl])
        grows.append(grow[:, sl])
    n = len(items)
    rng = range(n)
    kfs = [ks[t].astype(F32) for t in rng]
    gammas = [jnp.where(incl, jnp.exp(jnp.where(incl, gcols[t] - grows[t], 0.0)), 0.0) for t in rng]
    kbs = [kfs[t] * bcols[t] for t in rng]
    n_pows = [jnp.where(strict, -(_dot_nt(kbs[t], ks[t]) * gammas[t]), 0.0) for t in rng]
    t_invs = [eye + n_pows[t] for t in rng]
    for _ in range(5):
        n_pows = [_dot(n_pows[t], n_pows[t]) for t in rng]
        t_invs = [t_invs[t] + _dot(t_invs[t], n_pows[t]) for t in rng]
    e_gs = [jnp.exp(gcols[t]) for t in rng]
    wus = [_dot(t_invs[t], jnp.concatenate([kbs[t] * e_gs[t], vs[t] * bcols[t]], axis=1)) for t in rng]
    qks = [jnp.where(incl, _dot_nt(qs[t], ks[t]) * gammas[t], 0.0) for t in rng]
    g_lasts = [grows[t][:, c - 1:c] for t in rng]
    q_decs = [qs[t].astype(F32) * e_gs[t] for t in rng]
    k_decs = [kfs[t] * jnp.exp(g_lasts[t] - gcols[t]) for t in rng]
    decays = [jnp.exp(g_lasts[t]) for t in rng]

    states = [state_ref[i] for i in range(hb)]
    outs = [None] * n
    for ci in range(nc):
        idx = [ci * hb + i for i in range(hb)]
        v_news = [wus[t][:, dk:] - _dot(wus[t][:, :dk], states[i]) for i, t in enumerate(idx)]
        o_state = [_dot(q_decs[t], states[i]) for i, t in enumerate(idx)]
        for i, t in enumerate(idx):
            outs[t] = o_state[i] + _dot(qks[t], v_news[i])
        states = [states[i] * decays[t] + _dot_tn(k_decs[t], v_news[i]) for i, t in enumerate(idx)]
    for i in range(hb):
        state_ref[i] = states[i]

    for t, (ci, i) in enumerate(items):
        o = outs[t]
        o = o * lax.rsqrt(jnp.mean(o * o, axis=-1, keepdims=True) + EPS) * onorm_ref[...]
        z = z_ref[0, ci * c:(ci + 1) * c, i * dk:(i + 1) * dk].astype(F32)
        o_ref[0, ci * c:(ci + 1) * c, i * dk:(i + 1) * dk] = (o * _silu(z)).astype(o_ref.dtype)


def gdn_chunk(qk, v, z, gcc, betac, gcr, o_norm, *, n_heads, ts, heads_per_step):
    b, s, _ = v.shape
    dk = HEAD_DIM
    ts = _tile(s, ts)
    hb = _tile(n_heads, heads_per_step)
    groups = n_heads // hb
    assert ts % GDN_CHUNK == 0
    tok = lambda part: pl.BlockSpec((1, ts, hb * dk), lambda bi, hi, si, p=part: (bi, si, hi + p * groups))
    gate_col = pl.BlockSpec((1, ts, LANES), lambda bi, hi, si: (bi, si, 0))
    return pl.pallas_call(
        functools.partial(_gdn_chunk_kernel, heads_per_step=hb),
        grid=(b, groups, s // ts),
        in_specs=[tok(0), tok(1), tok(0), tok(0), gate_col, gate_col,
                  pl.BlockSpec((1, n_heads, ts), lambda bi, hi, si: (bi, 0, si)),
                  pl.BlockSpec((1, dk), lambda bi, hi, si: (0, 0))],
        out_specs=tok(0),
        out_shape=jax.ShapeDtypeStruct((b, s, n_heads * dk), BF16),
        scratch_shapes=[pltpu.VMEM((hb, dk, dk), F32)],
        compiler_params=_params("parallel", "parallel", "arbitrary"),
        name="gdn_chunk",
    )(qk, qk, v, z, gcc, betac, gcr, o_norm.reshape(1, dk).astype(F32))


def gdn_split_weights(w_in, n_heads):
    wide = 4 * n_heads * HEAD_DIM
    pad = ((0, 0), (0, 0), (0, LANES - n_heads))
    w_a = jnp.pad(w_in[:, :, wide:wide + n_heads], pad)
    w_b = jnp.pad(w_in[:, :, wide + n_heads:], pad)
    return w_in[:, :, :wide].astype(BF16), jnp.concatenate([w_a, w_b], axis=2).astype(BF16)


def gated_deltanet(x, xb, ss, w_big, w_small, layer, conv_w, a_log, dt_bias, o_norm, w_out,
                   next_gain, batch):
    t, d = x.shape
    n_heads = a_log.shape[0]
    s = t // batch
    hd = n_heads * HEAD_DIM
    conv_w = conv_w.astype(F32)
    tn = _tile(hd, 1024)
    qk_scale = jnp.concatenate([jnp.full((1, hd), HEAD_DIM ** -0.5, F32), jnp.ones((1, hd), F32)], axis=1)
    (qk,) = projection(xb, ss, w_big, layer, name="gdn_qk_proj", col0=0, n_cols=2 * hd, n_streams=1,
                       epilogue=_gdn_qk_epilogue, outs=[_row_col_tile(t, 2 * hd, BF16, tn)], tn=tn,
                       extras=[_col_vec(conv_w[:, :2 * hd], tn), _col_vec(qk_scale, tn)],
                       halo=True, seq_len=s)
    (v,) = projection(xb, ss, w_big, layer, name="gdn_v_proj", col0=2 * hd, n_cols=hd, n_streams=1,
                      epilogue=_gdn_v_epilogue, outs=[_row_col_tile(t, hd, BF16, tn)], tn=tn,
                      extras=[_col_vec(conv_w[:, 2 * hd:], tn)], halo=True, seq_len=s)
    (z,) = projection(xb, ss, w_big, layer, name="gdn_z_proj", col0=3 * hd, n_cols=hd, n_streams=1,
                      epilogue=_plain_epilogue, outs=[_row_col_tile(t, hd, BF16, tn)], tn=tn)
    (ab,) = projection(xb, ss, w_small, layer, name="gdn_ab_proj", col0=0, n_cols=2 * LANES,
                       n_streams=1, epilogue=_plain_epilogue,
                       outs=[_row_col_tile(t, 2 * LANES, F32, 2 * LANES)], tn=2 * LANES)
    gcc, betac = gdn_gates(ab, a_log, dt_bias, tb=512)
    gcc = gcc.reshape(batch, s, LANES)
    betac = betac.reshape(batch, s, LANES)
    gcr = jnp.swapaxes(gcc[:, :, :n_heads], 1, 2)
    o = gdn_chunk(qk.reshape(batch, s, 2 * hd), v.reshape(batch, s, hd), z.reshape(batch, s, hd),
                  gcc, betac, gcr, o_norm, n_heads=n_heads, ts=256, heads_per_step=4)
    return matmul_residual(o.reshape(t, hd), w_out, layer, x, next_gain, name="gdn_out_proj", scale=1.0)


def _shortconv_epilogue(accs, extra_refs, out_refs, cols):
    gb, gc, xv = accs
    conv = _conv_rows(gc * xv, extra_refs[0], cols)
    out_refs[0][:, cols] = (gb[HALO_ROWS:] * conv).astype(out_refs[0].dtype)


def short_conv(x, xb, ss, w_in, layer, conv_w, w_out, next_gain, batch):
    t, d = x.shape
    tn = 256
    (y,) = projection(xb, ss, w_in, layer, name="sc_in_proj", col0=0, n_cols=d, n_streams=3,
                      epilogue=_shortconv_epilogue, outs=[_row_col_tile(t, d, BF16, tn)], tn=tn,
                      extras=[_col_vec(conv_w.astype(F32), tn)], halo=True, seq_len=t // batch)
    return matmul_residual(y, w_out, layer, x, next_gain, name="sc_out_proj", scale=1.0)


def _rope_table_kernel(pos_ref, freq_ref, cos_ref, sin_ref):
    ang = pos_ref[0].astype(F32) * freq_ref[...]
    lane = lax.broadcasted_iota(jnp.int32, ang.shape, 1)
    sign = jnp.where(lane < HEAD_DIM // 2, -1.0, 1.0)
    cos_ref[0] = jnp.cos(ang)
    sin_ref[0] = jnp.sin(ang) * sign


def rope_tables(positions, *, ts):
    b, s = positions.shape
    ts = _tile(s, ts)
    half = HEAD_DIM // 2
    inv_freq = ROPE_THETA ** (-jnp.arange(half, dtype=F32) / half)
    freq = jnp.concatenate([inv_freq, inv_freq]).reshape(1, HEAD_DIM)
    blk = pl.BlockSpec((1, ts, HEAD_DIM), lambda bi, si: (bi, si, 0))
    return pl.pallas_call(
        _rope_table_kernel,
        grid=(b, s // ts),
        in_specs=[pl.BlockSpec((1, ts, 1), lambda bi, si: (bi, si, 0)),
                  pl.BlockSpec((1, HEAD_DIM), lambda bi, si: (0, 0))],
        out_specs=[blk, blk],
        out_shape=[jax.ShapeDtypeStruct((b, s, HEAD_DIM), F32)] * 2,
        compiler_params=_params("parallel", "parallel"),
        name="rope_tables",
    )(positions.reshape(b, s, 1), freq)


def _norm_rope(seg, cos, sin, gain):
    seg = seg * lax.rsqrt(jnp.mean(seg * seg, axis=-1, keepdims=True) + EPS) * gain
    return seg * cos + pltpu.roll(seg, HEAD_DIM // 2, axis=1) * sin


def _moba_q_epilogue(accs, extra_refs, out_refs, cols):
    cos_ref, sin_ref, gain_ref = extra_refs
    cos, sin, gain = cos_ref[...], sin_ref[...], gain_ref[...]
    acc = accs[0]
    for _, hl in _heads(cols):
        seg = acc[:, hl.start - cols.start:hl.stop - cols.start]
        qh = _norm_rope(seg, cos, sin, gain) * (HEAD_DIM ** -0.5 * LOG2_E)
        out_refs[0][:, hl] = qh.astype(out_refs[0].dtype)


def _moba_k_epilogue(accs, extra_refs, out_refs, cols):
    cos_ref, sin_ref, gain_ref = extra_refs
    cos, sin, gain = cos_ref[...], sin_ref[...], gain_ref[...]
    ko_ref, km_ref = out_refs
    acc = accs[0]
    bs = MOBA_BLOCK
    for h, hl in _heads(cols):
        kh = _norm_rope(acc[:, hl.start - cols.start:hl.stop - cols.start], cos, sin, gain)
        for g in range(acc.shape[0] // bs):
            blk = kh[g * bs:(g + 1) * bs]
            ko_ref[0, h, g] = blk.astype(ko_ref.dtype)
            km_ref[g, :, hl] = jnp.mean(blk, axis=0, keepdims=True)


def _moba_v_epilogue(accs, extra_refs, out_refs, cols):
    vto_ref = out_refs[0]
    acc = accs[0]
    bs, dh = MOBA_BLOCK, HEAD_DIM
    sub = lax.broadcasted_iota(jnp.int32, (SUM_ROWS, bs), 0)
    ones_rows = jnp.where(sub == 0, 1.0, 0.0).astype(vto_ref.dtype)
    for h, hl in _heads(cols):
        for g in range(acc.shape[0] // bs):
            blk = acc[g * bs:(g + 1) * bs, hl.start - cols.start:hl.stop - cols.start]
            vto_ref[0, h, g, 0:dh, :] = blk.T.astype(vto_ref.dtype)
            vto_ref[0, h, g, dh:dh + SUM_ROWS, :] = ones_rows


def _moba_attn_kernel(q_ref, k_ref, vt_ref, km_ref, o_ref, bias_ref, *, group, slab):
    bs, dh = MOBA_BLOCK, HEAD_DIM
    nb = km_ref.shape[2]
    qi = pl.program_id(2)
    q = q_ref[0]

    gate = None
    for part in _split3(km_ref[0, 0]):
        term = _dot_nt(part, q)
        gate = term if gate is None else gate + term
    blk = lax.broadcasted_iota(jnp.int32, (nb, bs), 0)
    past = blk < qi
    gate = jnp.where(past, gate, -jnp.inf)
    blk_f = blk.astype(F32)
    bias = jnp.full((nb, bs), -jnp.inf, F32)
    for _ in range(MOBA_TOPK):
        top = jnp.max(gate, axis=0, keepdims=True)
        first = jnp.min(jnp.where(gate == top, blk_f, float(nb)), axis=0, keepdims=True)
        hit = (blk_f == first) & (top > -jnp.inf)
        bias = jnp.where(hit, 0.0, bias)
        gate = jnp.where(hit, -jnp.inf, gate)
    bias_ref[...] = bias

    def scores(j):
        return _dot_nt(k_ref[0, 0, j], q)

    def slab_scores(j):
        return _dot_nt(k_ref[0, 0, pl.ds(j, slab)].reshape(slab * bs, dh), q)

    def pieces(s_t):
        m_blk = jnp.max(s_t, axis=0, keepdims=True)
        return m_blk, jnp.exp2(s_t - m_blk).astype(BF16)

    def values(j, p):
        return jnp.dot(vt_ref[0, 0, j], p, preferred_element_type=F32)[:dh + 8]

    def merge(m, acc, parts):
        m_new = functools.reduce(jnp.maximum, [m] + [mb for mb, _ in parts])
        acc = jnp.exp2(m - m_new) * acc
        for mb, ab in parts:
            acc = acc + jnp.exp2(mb - m_new) * ab
        return m_new, acc

    key_pos = lax.broadcasted_iota(jnp.int32, (bs, bs), 0)
    qry_pos = lax.broadcasted_iota(jnp.int32, (bs, bs), 1)
    m, p = pieces(jnp.where(key_pos <= qry_pos, scores(qi), -jnp.inf))
    acc = values(qi, p)

    def body(it, carry):
        m, acc = carry
        j0 = it * group
        slabs = [slab_scores(j0 + u * slab) for u in range(group // slab)]
        mp = [pieces(slabs[b // slab][(b % slab) * bs:(b % slab + 1) * bs]) for b in range(group)]
        parts = [(mp[b][0] + bias_ref[pl.ds(j0 + b, 1), :], values(j0 + b, mp[b][1]))
                 for b in range(group)]
        return merge(m, acc, parts)

    m, acc = lax.fori_loop(0, (qi + group - 1) // group, body, (m, acc))
    o_ref[0] = (acc[:dh] / acc[dh:dh + 1]).T.astype(o_ref.dtype)


def moba_attn(q, k, vt, kmean):
    b, s, d = q.shape
    _, n_heads, nb, bs, dh = k.shape
    group = _tile(nb, 8)
    slab = _tile(group, 2)
    return pl.pallas_call(
        functools.partial(_moba_attn_kernel, group=group, slab=slab),
        grid=(b, n_heads, nb),
        in_specs=[pl.BlockSpec((1, bs, dh), lambda bi, hi, qi: (bi, qi, hi)),
                  pl.BlockSpec((1, 1, nb, bs, dh), lambda bi, hi, qi: (bi, hi, 0, 0, 0)),
                  pl.BlockSpec((1, 1, nb, dh + SUM_ROWS, bs), lambda bi, hi, qi: (bi, hi, 0, 0, 0)),
                  pl.BlockSpec((1, 1, nb, dh), lambda bi, hi, qi: (bi, hi, 0, 0))],
        out_specs=pl.BlockSpec((1, bs, dh), lambda bi, hi, qi: (bi, qi, hi)),
        out_shape=jax.ShapeDtypeStruct((b, s, d), BF16),
        scratch_shapes=[pltpu.VMEM((nb, bs), F32)],
        compiler_params=_params("parallel", "parallel", "arbitrary"),
        name="moba_attn",
    )(q, k, vt, kmean)


def moba_attention(x, xb, ss, positions, w_in, layer, q_norm, k_norm, w_out, next_gain):
    t, d = x.shape
    batch, s = positions.shape
    n_heads = d // HEAD_DIM
    dh, bs = HEAD_DIM, MOBA_BLOCK
    assert s % bs == 0
    nb = s // bs
    tm = _tile(t, ROW_TILE)
    assert tm % bs == 0 and s % tm == 0
    tn = _tile(d, 1024)
    hpt, bpt, tiles_per_seq = tn // dh, tm // bs, s // tm
    cos, sin = rope_tables(positions, ts=512)
    rows = pl.BlockSpec((tm, dh), lambda i, j: (i, 0))
    vec = pl.BlockSpec((1, dh), lambda i, j: (0, 0))
    rope_extras = lambda gain: [(cos.reshape(t, dh), rows), (sin.reshape(t, dh), rows),
                                (gain.reshape(1, dh).astype(F32), vec)]
    blocked = lambda i, j: (i // tiles_per_seq, j, i % tiles_per_seq, 0, 0)
    (q,) = projection(xb, ss, w_in, layer, name="moba_q_proj", col0=0, n_cols=d, n_streams=1,
                      epilogue=_moba_q_epilogue, outs=[_row_col_tile(t, d, BF16, tn)], tn=tn,
                      extras=rope_extras(q_norm))
    k, kmean = projection(
        xb, ss, w_in, layer, name="moba_k_proj", col0=d, n_cols=d, n_streams=1,
        epilogue=_moba_k_epilogue, tn=tn, extras=rope_extras(k_norm),
        outs=[(jax.ShapeDtypeStruct((batch, n_heads, nb, bs, dh), BF16),
               pl.BlockSpec((1, hpt, bpt, bs, dh), blocked)),
              (jax.ShapeDtypeStruct((t // bs, 1, d), F32),
               pl.BlockSpec((bpt, 1, tn), lambda i, j: (i, 0, j)))])
    (vt,) = projection(
        xb, ss, w_in, layer, name="moba_v_proj", col0=2 * d, n_cols=d, n_streams=1,
        epilogue=_moba_v_epilogue, tn=tn,
        outs=[(jax.ShapeDtypeStruct((batch, n_heads, nb, dh + SUM_ROWS, bs), BF16),
               pl.BlockSpec((1, hpt, bpt, dh + SUM_ROWS, bs), blocked))])
    kmean = kmean.reshape(batch, nb, n_heads, dh).transpose(0, 2, 1, 3)
    o = moba_attn(q.reshape(batch, s, d), k, vt, kmean)
    return matmul_residual(o.reshape(t, d), w_out, layer, x, next_gain, name="moba_out_proj", scale=1.0)


def swiglu_half_step(x, xb, ss, w_in, w_out, layer, next_gain):
    t = x.shape[0]
    f = w_in.shape[2] // 2
    tn = 512
    (act,) = projection(xb, ss, w_in, layer, name="ffn_in", col0=0, n_cols=f, n_streams=2,
                        epilogue=_swiglu_epilogue, outs=[_row_col_tile(t, f, BF16, tn)], tn=tn)
    return matmul_residual(act, w_out, layer, x, next_gain, name="ffn_out", scale=0.5)


def kernel(x, positions, norm_ffn1, norm_mix, norm_ffn2, ffn1_w_in, ffn1_w_out, ffn2_w_in, ffn2_w_out, gdn_w_in, gdn_conv_w, gdn_a_log, gdn_dt_bias, gdn_out_norm, gdn_w_out, sc_w_in, sc_conv_w, sc_w_out, moba_w_in, moba_q_norm, moba_k_norm, moba_w_out):
    batch, s, d = x.shape
    depth = norm_ffn1.shape[0]
    ffn1_w_in, ffn1_w_out, ffn2_w_in, ffn2_w_out, gdn_w_out, sc_w_in, sc_w_out, moba_w_in, moba_w_out = (
        w.astype(BF16) for w in (ffn1_w_in, ffn1_w_out, ffn2_w_in, ffn2_w_out, gdn_w_out, sc_w_in,
                                 sc_w_out, moba_w_in, moba_w_out))
    gdn_w_big, gdn_w_small = gdn_split_weights(gdn_w_in, gdn_a_log.shape[1])
    xt = x.reshape(batch * s, d)
    xb, ss = scale_rows(xt, norm_ffn1[0], tm=256)
    for i in range(depth):
        kind, j = i % 3, i // 3
        after_layer = norm_ffn1[i + 1] if i + 1 < depth else None
        xt, xb, ss = swiglu_half_step(xt, xb, ss, ffn1_w_in, ffn1_w_out, i, norm_mix[i])
        if kind == 0:
            xt, xb, ss = gated_deltanet(xt, xb, ss, gdn_w_big, gdn_w_small, j, gdn_conv_w[j], gdn_a_log[j],
                                        gdn_dt_bias[j], gdn_out_norm[j], gdn_w_out, norm_ffn2[i], batch)
        elif kind == 1:
            xt, xb, ss = short_conv(xt, xb, ss, sc_w_in, j, sc_conv_w[j], sc_w_out, norm_ffn2[i], batch)
        else:
            xt, xb, ss = moba_attention(xt, xb, ss, positions, moba_w_in, j, moba_q_norm[j],
                                        moba_k_norm[j], moba_w_out, norm_ffn2[i])
        xt, xb, ss = swiglu_half_step(xt, xb, ss, ffn2_w_in, ffn2_w_out, i, after_layer)
    return xt.reshape(batch, s, d)
```

```python
import functools

import jax
import jax.numpy as jnp
from jax import lax
from jax.experimental import pallas as pl
from jax.experimental.pallas import tpu as pltpu

F32 = jnp.float32
BF16 = jnp.bfloat16

EPS = 1e-6
HEAD_DIM = 128
GDN_CHUNK = 64
MOBA_BLOCK = 256
MOBA_TOPK = 3
ROPE_THETA = 10000.0
LOG2_E = 1.4426950408889634
SUM_ROWS = 16

LANES = 128
MXU_COLS = 256
HALO_ROWS = 16
VMEM_LIMIT = 56 * 1024 * 1024
ROW_TILE = 1024
RES_ROW_TILE = 512
RES_COL_TILE = 1024
CONV_ROW_CHUNK = 32
NORM_CONV_ROW_CHUNK = 128


def _tile(dim, want):
    t = min(dim, want)
    while dim % t:
        t //= 2
    return t


def _params(*sem):
    return pltpu.CompilerParams(dimension_semantics=sem, vmem_limit_bytes=VMEM_LIMIT)


def _dot(a, b):
    return jnp.dot(a.astype(BF16), b.astype(BF16), preferred_element_type=F32)


def _dot_nt(a, b):
    return lax.dot_general(a.astype(BF16), b.astype(BF16), (((1,), (1,)), ((), ())),
                           preferred_element_type=F32)


def _dot_tn(a, b):
    return lax.dot_general(a.astype(BF16), b.astype(BF16), (((0,), (0,)), ((), ())),
                           preferred_element_type=F32)


def _split3(x):
    hi = x.astype(BF16)
    r1 = x - hi.astype(F32)
    mid = r1.astype(BF16)
    lo = (r1 - mid.astype(F32)).astype(BF16)
    return hi, mid, lo


def _silu(x):
    return x * jax.nn.sigmoid(x)


def _fold_lanes(x):
    parts = [x[:, c * LANES:(c + 1) * LANES] for c in range(x.shape[1] // LANES)]
    return functools.reduce(jnp.add, parts)


def _scale_rows_kernel(x_ref, g_ref, xb_ref, ss_ref):
    x = x_ref[...]
    xb_ref[...] = (x * g_ref[...]).astype(BF16)
    ss_ref[...] = _fold_lanes(x * x)


def scale_rows(x, gain, *, tm):
    t, d = x.shape
    tm = _tile(t, tm)
    return pl.pallas_call(
        _scale_rows_kernel,
        grid=(t // tm,),
        in_specs=[pl.BlockSpec((tm, d), lambda i: (i, 0)), pl.BlockSpec((1, d), lambda i: (0, 0))],
        out_specs=[pl.BlockSpec((tm, d), lambda i: (i, 0)), pl.BlockSpec((tm, LANES), lambda i: (i, 0))],
        out_shape=[jax.ShapeDtypeStruct((t, d), BF16), jax.ShapeDtypeStruct((t, LANES), F32)],
        compiler_params=_params("parallel"),
        name="scale_rows",
    )(x, gain.reshape(1, d).astype(F32))


def _mm_residual_kernel(*refs, scale, emit_next):
    if emit_next:
        a_ref, w_ref, x_ref, g_ref, o_ref, xb_ref, ss_ref = refs
    else:
        a_ref, w_ref, x_ref, o_ref = refs
    xn = x_ref[...] + scale * jnp.dot(a_ref[...], w_ref[...], preferred_element_type=F32)
    o_ref[...] = xn
    if emit_next:
        xb_ref[...] = (xn * g_ref[...]).astype(BF16)
        part = _fold_lanes(xn * xn)

        @pl.when(pl.program_id(1) == 0)
        def _():
            ss_ref[...] = part

        @pl.when(pl.program_id(1) != 0)
        def _():
            ss_ref[...] += part


def matmul_residual(a, w, layer, x, next_gain, *, name, scale):
    t, k = a.shape
    n = w.shape[2]
    tm, tn = _tile(t, RES_ROW_TILE), _tile(n, RES_COL_TILE)
    emit_next = next_gain is not None
    tile = pl.BlockSpec((tm, tn), lambda i, j: (i, j))
    in_specs = [pl.BlockSpec((tm, k), lambda i, j: (i, 0)),
                pl.BlockSpec((pl.Squeezed(), k, tn), lambda i, j: (layer, 0, j)),
                tile]
    args = [a, w, x]
    out_specs, out_shape = [tile], [jax.ShapeDtypeStruct((t, n), F32)]
    if emit_next:
        in_specs.append(pl.BlockSpec((1, tn), lambda i, j: (0, j)))
        args.append(next_gain.reshape(1, n).astype(F32))
        out_specs += [tile, pl.BlockSpec((tm, LANES), lambda i, j: (i, 0))]
        out_shape += [jax.ShapeDtypeStruct((t, n), BF16), jax.ShapeDtypeStruct((t, LANES), F32)]
    outs = pl.pallas_call(
        functools.partial(_mm_residual_kernel, scale=scale, emit_next=emit_next),
        grid=(t // tm, n // tn),
        in_specs=in_specs,
        out_specs=out_specs,
        out_shape=out_shape,
        compiler_params=_params("parallel", "arbitrary"),
        name=name,
    )(*args)
    return tuple(outs) if emit_next else (outs[0], None, None)


def _proj_kernel(*refs, n_streams, n_extra, n_out, epilogue, halo, lag, seq_len, k_dim, nj, n_tiles):
    xb_ref, ss_ref = refs[0], refs[1]
    pos = 2
    if halo:
        xh_ref, sh_ref = refs[2], refs[3]
        pos = 4
    w_refs = refs[pos:pos + n_streams]
    pos += n_streams
    extra_refs = refs[pos:pos + n_extra]
    pos += n_extra
    out_refs = refs[pos:pos + n_out]
    pos += n_out
    scratch = list(refs[pos:])
    r_ref = scratch.pop(0)
    a_ref = scratch.pop(0) if halo else None
    tm = xb_ref.shape[0]
    tn = w_refs[0].shape[1]
    hr = HALO_ROWS if halo else 0
    step = pl.program_id(0)
    tile = jnp.minimum(step, n_tiles - 1)

    def row_factor(ss):
        return lax.rsqrt(jnp.sum(ss, axis=-1, keepdims=True) * (1.0 / k_dim) + EPS)

    seq_start = ((tile // nj) * tm) % seq_len == 0 if halo else None

    @pl.when(tile % nj == 0)
    def _():
        r_ref[hr:hr + tm, :] = row_factor(ss_ref[...])
        if halo:
            a_ref[hr:hr + tm, :] = xb_ref[...]
            a_ref[0:hr, :] = xh_ref[...]
            r_ref[0:hr, :] = jnp.where(seq_start, 0.0, row_factor(sh_ref[...]))

    def products(cols):
        a = a_ref[...] if halo else xb_ref[...]
        r = r_ref[...]
        return [jnp.dot(a, w[:, cols], preferred_element_type=F32) * r for w in w_refs]

    if not lag:
        strip = min(tn, MXU_COLS)
        for c0 in range(0, tn, strip):
            cols = slice(c0, c0 + strip)
            epilogue(products(cols), extra_refs, out_refs, cols)
        return

    slots = [scratch[:n_streams], scratch[n_streams:]]
    everything = slice(0, tn)

    @pl.when(step == 0)
    def _():
        for acc_ref in slots[1]:
            acc_ref[...] = jnp.zeros(acc_ref.shape, F32)

    def phase(cur, prev):
        epilogue(slots[prev], extra_refs, out_refs, everything)
        for acc_ref, acc in zip(slots[cur], products(everything)):
            acc_ref[...] = acc

    @pl.when(step % 2 == 0)
    def _():
        phase(0, 1)

    @pl.when(step % 2 == 1)
    def _():
        phase(1, 0)


def projection(xb, ss, w, layer, *, name, col0, n_cols, n_streams, epilogue, outs, tn,
               extras=(), halo=False, lag=False, seq_len=None):
    t, k = xb.shape
    tm, tn = _tile(t, ROW_TILE), _tile(n_cols, tn)
    assert col0 % tn == 0
    nj, j0 = n_cols // tn, col0 // tn
    n_tiles = (t // tm) * nj
    hr = HALO_ROWS if halo else 0

    def cur(step):
        tile = jnp.minimum(step, n_tiles - 1)
        return tile // nj, tile % nj

    def out_tile(step):
        tile = jnp.maximum(step - 1, 0) if lag else step
        return tile // nj, tile % nj

    on_cur = lambda f: (lambda step: f(*cur(step)))
    on_out = lambda spec: pl.BlockSpec(spec.block_shape, lambda step, f=spec.index_map: f(*out_tile(step)))
    rows_mode = pl.Buffered(1) if (halo or tn >= 1024) else None
    in_specs = [pl.BlockSpec((tm, k), on_cur(lambda i, j: (i, 0)), pipeline_mode=rows_mode),
                pl.BlockSpec((tm, LANES), on_cur(lambda i, j: (i, 0)))]
    args = [xb, ss]
    scratch = [pltpu.VMEM((tm + hr, 1), F32)]
    if halo:
        assert seq_len % tm == 0 and tm % HALO_ROWS == 0
        per = tm // HALO_ROWS
        prev = on_cur(lambda i, j: (jnp.maximum(i * per - 1, 0), 0))
        in_specs += [pl.BlockSpec((HALO_ROWS, k), prev), pl.BlockSpec((HALO_ROWS, LANES), prev)]
        args += [xb, ss]
        scratch.append(pltpu.VMEM((tm + HALO_ROWS, k), BF16))
    if lag:
        scratch += [pltpu.VMEM((tm + hr, tn), F32)] * (2 * n_streams)
    for s in range(n_streams):
        in_specs.append(pl.BlockSpec((pl.Squeezed(), k, tn),
                                     on_cur(lambda i, j, s=s: (layer, 0, j0 + j + s * nj))))
        args.append(w)
    for arr, spec in extras:
        in_specs.append(on_out(spec))
        args.append(arr)
    body = functools.partial(_proj_kernel, n_streams=n_streams, n_extra=len(extras), n_out=len(outs),
                             epilogue=epilogue, halo=halo, lag=lag, seq_len=seq_len, k_dim=k, nj=nj,
                             n_tiles=n_tiles)
    return pl.pallas_call(
        body,
        grid=(n_tiles + int(lag),),
        in_specs=in_specs,
        out_specs=[on_out(spec) for _, spec in outs],
        out_shape=[shape for shape, _ in outs],
        scratch_shapes=scratch,
        compiler_params=_params("arbitrary"),
        name=name,
    )(*args)


def _row_col_tile(t, n, dtype, tn):
    tm, tn = _tile(t, ROW_TILE), _tile(n, tn)
    return jax.ShapeDtypeStruct((t, n), dtype), pl.BlockSpec((tm, tn), lambda i, j: (i, j))


def _col_vec(v, tn):
    return v, pl.BlockSpec((v.shape[0], tn), lambda i, j: (0, j))


def _conv_row_chunks(exts, w_ref, cols, row_chunk=CONV_ROW_CHUNK):
    width = w_ref.shape[0]
    tm = exts[0].shape[0] - HALO_ROWS
    chunk = _tile(tm, row_chunk)
    for r0 in range(0, tm, chunk):
        acc = None
        for tap in range(width):
            start = HALO_ROWS - (width - 1) + tap + r0
            term = functools.reduce(jnp.multiply, [e[start:start + chunk, :] for e in exts])
            term = term * w_ref[tap:tap + 1, cols]
            acc = term if acc is None else acc + term
        yield slice(r0, r0 + chunk), acc


def _heads(cols):
    return [(c // HEAD_DIM, slice(c, c + HEAD_DIM)) for c in range(cols.start, cols.stop, HEAD_DIM)]


def _swiglu_epilogue(accs, extra_refs, out_refs, cols):
    gate, up = accs
    out_refs[0][:, cols] = (_silu(gate) * up).astype(out_refs[0].dtype)


def _plain_epilogue(accs, extra_refs, out_refs, cols):
    out_refs[0][:, cols] = accs[0].astype(out_refs[0].dtype)


def _gdn_qk_epilogue(accs, extra_refs, out_refs, cols):
    conv_ref, scale_ref = extra_refs
    for rows, conv in _conv_row_chunks(accs, conv_ref, cols, NORM_CONV_ROW_CHUNK):
        y = _silu(conv)
        for _, hl in _heads(cols):
            seg = y[:, hl.start - cols.start:hl.stop - cols.start]
            seg = seg * lax.rsqrt(jnp.sum(seg * seg, axis=-1, keepdims=True) + EPS) * scale_ref[:, hl]
            out_refs[0][rows, hl] = seg.astype(out_refs[0].dtype)


def _gdn_v_epilogue(accs, extra_refs, out_refs, cols):
    for rows, conv in _conv_row_chunks(accs, extra_refs[0], cols):
        out_refs[0][rows, cols] = _silu(conv).astype(out_refs[0].dtype)


def _gdn_gates_kernel(a_ref, bt_ref, alog_ref, dtb_ref, gc_ref, beta_ref):
    c = GDN_CHUNK
    xs = a_ref[...] + dtb_ref[...]
    softplus = jnp.maximum(xs, 0.0) + jnp.log(1.0 + jnp.exp(-jnp.abs(xs)))
    g = -jnp.exp(alog_ref[...]) * softplus
    row = lax.broadcasted_iota(jnp.int32, (c, c), 0)
    col = lax.broadcasted_iota(jnp.int32, (c, c), 1)
    tri = (row >= col).astype(BF16)
    for r in range(0, g.shape[0], c):
        hi, mid, lo = _split3(g[r:r + c, :])
        gc_ref[r:r + c, :] = (jnp.dot(tri, hi, preferred_element_type=F32)
                              + jnp.dot(tri, mid, preferred_element_type=F32)
                              + jnp.dot(tri, lo, preferred_element_type=F32))
    beta_ref[...] = jax.nn.sigmoid(bt_ref[...])


def gdn_gates(ab, a_log, dt_bias, *, tb):
    t = ab.shape[0]
    h = a_log.shape[0]
    tb = _tile(t, tb)
    pad = lambda v: jnp.pad(v.astype(F32), (0, LANES - h)).reshape(1, LANES)
    row_blk = lambda j: pl.BlockSpec((tb, LANES), lambda i, j=j: (i, j))
    vec = pl.BlockSpec((1, LANES), lambda i: (0, 0))
    return pl.pallas_call(
        _gdn_gates_kernel,
        grid=(t // tb,),
        in_specs=[row_blk(0), row_blk(1), vec, vec],
        out_specs=[row_blk(0), row_blk(0)],
        out_shape=[jax.ShapeDtypeStruct((t, LANES), F32)] * 2,
        compiler_params=_params("parallel"),
        name="gdn_gates",
    )(ab, ab, pad(a_log), pad(dt_bias))


def _gdn_chunk_kernel(q_ref, k_ref, v_ref, z_ref, gcc_ref, bc_ref, gcr_ref, onorm_ref, o_ref,
                      state_ref, *, heads_per_step):
    c = GDN_CHUNK
    dk = HEAD_DIM
    ts = q_ref.shape[1]

    @pl.when(pl.program_id(2) == 0)
    def _():
        state_ref[...] = jnp.zeros(state_ref.shape, F32)

    row = lax.broadcasted_iota(jnp.int32, (c, c), 0)
    col = lax.broadcasted_iota(jnp.int32, (c, c), 1)
    incl = row >= col
    strict = row > col
    eye = (row == col).astype(F32)
    lane = lax.broadcasted_iota(jnp.int32, (ts, LANES), 1)
    hb, nc = heads_per_step, ts // c

    per_head = []
    for i in range(hb):
        head = pl.program_id(1) * hb + i
        pick = lane == head
        gcol = jnp.sum(jnp.where(pick, gcc_ref[0], 0.0), axis=-1, keepdims=True)
        bcol = jnp.sum(jnp.where(pick, bc_ref[0], 0.0), axis=-1, keepdims=True)
        grow = gcr_ref[0, pl.ds(head, 1), :]
        per_head.append((gcol, bcol, grow))
    items = [(ci, i) for ci in range(nc) for i in range(hb)]
    qs, ks, vs, gcols, bcols, grows = [], [], [], [], [], []
    for ci, i in items:
        gcol, bcol, grow = per_head[i]
        sl = slice(ci * c, (ci + 1) * c)
        hl = slice(i * dk, (i + 1) * dk)
        qs.append(q_ref[0, sl, hl])
        ks.append(k_ref[0, sl, hl])
        vs.append(v_ref[0, sl, hl].astype(F32))
        gcols.append(gcol[sl])
        bcols.append(bcol[sl])
        grows.append(grow[:, sl])
    n = len(items)
    rng = range(n)
    kfs = [ks[t].astype(F32) for t in rng]
    gammas = [jnp.where(incl, jnp.exp(jnp.where(incl, gcols[t] - grows[t], 0.0)), 0.0) for t in rng]
    kbs = [kfs[t] * bcols[t] for t in rng]
    n_pows = [jnp.where(strict, -(_dot_nt(kbs[t], ks[t]) * gammas[t]), 0.0) for t in rng]
    t_invs = [eye + n_pows[t] for t in rng]
    for _ in range(5):
        n_pows = [_dot(n_pows[t], n_pows[t]) for t in rng]
        t_invs = [t_invs[t] + _dot(t_invs[t], n_pows[t]) for t in rng]
    e_gs = [jnp.exp(gcols[t]) for t in rng]
    wus = [_dot(t_invs[t], jnp.concatenate([kbs[t] * e_gs[t], vs[t] * bcols[t]], axis=1)) for t in rng]
    qks = [jnp.where(incl, _dot_nt(qs[t], ks[t]) * gammas[t], 0.0) for t in rng]
    g_lasts = [grows[t][:, c - 1:c] for t in rng]
    q_decs = [qs[t].astype(F32) * e_gs[t] for t in rng]
    k_decs = [kfs[t] * jnp.exp(g_lasts[t] - gcols[t]) for t in rng]
    decays = [jnp.exp(g_lasts[t]) for t in rng]

    states = [state_ref[i] for i in range(hb)]
    outs = [None] * n
    for ci in range(nc):
        idx = [ci * hb + i for i in range(hb)]
        v_news = [wus[t][:, dk:] - _dot(wus[t][:, :dk], states[i]) for i, t in enumerate(idx)]
        o_state = [_dot(q_decs[t], states[i]) for i, t in enumerate(idx)]
        for i, t in enumerate(idx):
            outs[t] = o_state[i] + _dot(qks[t], v_news[i])
        states = [states[i] * decays[t] + _dot_tn(k_decs[t], v_news[i]) for i, t in enumerate(idx)]
    for i in range(hb):
        state_ref[i] = states[i]

    for t, (ci, i) in enumerate(items):
        o = outs[t]
        o = o * lax.rsqrt(jnp.mean(o * o, axis=-1, keepdims=True) + EPS) * onorm_ref[...]
        z = z_ref[0, ci * c:(ci + 1) * c, i * dk:(i + 1) * dk].astype(F32)
        o_ref[0, ci * c:(ci + 1) * c, i * dk:(i + 1) * dk] = (o * _silu(z)).astype(o_ref.dtype)


def gdn_chunk(qk, v, z, gcc, betac, gcr, o_norm, *, n_heads, ts, heads_per_step):
    b, s, _ = v.shape
    dk = HEAD_DIM
    ts = _tile(s, ts)
    hb = _tile(n_heads, heads_per_step)
    groups = n_heads // hb
    assert ts % GDN_CHUNK == 0
    tok = lambda part: pl.BlockSpec((1, ts, hb * dk), lambda bi, hi, si, p=part: (bi, si, hi + p * groups))
    gate_col = pl.BlockSpec((1, ts, LANES), lambda bi, hi, si: (bi, si, 0))
    return pl.pallas_call(
        functools.partial(_gdn_chunk_kernel, heads_per_step=hb),
        grid=(b, groups, s // ts),
        in_specs=[tok(0), tok(1), tok(0), tok(0), gate_col, gate_col,
                  pl.BlockSpec((1, n_heads, ts), lambda bi, hi, si: (bi, 0, si)),
                  pl.BlockSpec((1, dk), lambda bi, hi, si: (0, 0))],
        out_specs=tok(0),
        out_shape=jax.ShapeDtypeStruct((b, s, n_heads * dk), BF16),
        scratch_shapes=[pltpu.VMEM((hb, dk, dk), F32)],
        compiler_params=_params("parallel", "parallel", "arbitrary"),
        name="gdn_chunk",
    )(qk, qk, v, z, gcc, betac, gcr, o_norm.reshape(1, dk).astype(F32))


def gdn_split_weights(w_in, n_heads):
    wide = 4 * n_heads * HEAD_DIM
    pad = ((0, 0), (0, 0), (0, LANES - n_heads))
    w_a = jnp.pad(w_in[:, :, wide:wide + n_heads], pad)
    w_b = jnp.pad(w_in[:, :, wide + n_heads:], pad)
    return w_in[:, :, :wide].astype(BF16), jnp.concatenate([w_a, w_b], axis=2).astype(BF16)


def gated_deltanet(x, xb, ss, w_big, w_small, layer, conv_w, a_log, dt_bias, o_norm, w_out,
                   next_gain, batch):
    t, d = x.shape
    n_heads = a_log.shape[0]
    s = t // batch
    hd = n_heads * HEAD_DIM
    conv_w = conv_w.astype(F32)
    tn = _tile(hd, 512)
    qk_scale = jnp.concatenate([jnp.full((1, hd), HEAD_DIM ** -0.5, F32), jnp.ones((1, hd), F32)], axis=1)
    (qk,) = projection(xb, ss, w_big, layer, name="gdn_qk_proj", col0=0, n_cols=2 * hd, n_streams=1,
                       epilogue=_gdn_qk_epilogue, outs=[_row_col_tile(t, 2 * hd, BF16, tn)], tn=tn,
                       extras=[_col_vec(conv_w[:, :2 * hd], tn), _col_vec(qk_scale, tn)],
                       halo=True, lag=True, seq_len=s)
    (v,) = projection(xb, ss, w_big, layer, name="gdn_v_proj", col0=2 * hd, n_cols=hd, n_streams=1,
                      epilogue=_gdn_v_epilogue, outs=[_row_col_tile(t, hd, BF16, tn)], tn=tn,
                      extras=[_col_vec(conv_w[:, 2 * hd:], tn)], halo=True, lag=True, seq_len=s)
    (z,) = projection(xb, ss, w_big, layer, name="gdn_z_proj", col0=3 * hd, n_cols=hd, n_streams=1,
                      epilogue=_plain_epilogue, outs=[_row_col_tile(t, hd, BF16, tn)], tn=tn)
    (ab,) = projection(xb, ss, w_small, layer, name="gdn_ab_proj", col0=0, n_cols=2 * LANES,
                       n_streams=1, epilogue=_plain_epilogue,
                       outs=[_row_col_tile(t, 2 * LANES, F32, 2 * LANES)], tn=2 * LANES)
    gcc, betac = gdn_gates(ab, a_log, dt_bias, tb=512)
    gcc = gcc.reshape(batch, s, LANES)
    betac = betac.reshape(batch, s, LANES)
    gcr = jnp.swapaxes(gcc[:, :, :n_heads], 1, 2)
    o = gdn_chunk(qk.reshape(batch, s, 2 * hd), v.reshape(batch, s, hd), z.reshape(batch, s, hd),
                  gcc, betac, gcr, o_norm, n_heads=n_heads, ts=256, heads_per_step=4)
    return matmul_residual(o.reshape(t, hd), w_out, layer, x, next_gain, name="gdn_out_proj", scale=1.0)


def _shortconv_epilogue(accs, extra_refs, out_refs, cols):
    gb, gc, xv = accs
    for rows, conv in _conv_row_chunks([gc, xv], extra_refs[0], cols):
        gate = gb[HALO_ROWS + rows.start:HALO_ROWS + rows.stop, :]
        out_refs[0][rows, cols] = (gate * conv).astype(out_refs[0].dtype)


def short_conv(x, xb, ss, w_in, layer, conv_w, w_out, next_gain, batch):
    t, d = x.shape
    tn = 256
    (y,) = projection(xb, ss, w_in, layer, name="sc_in_proj", col0=0, n_cols=d, n_streams=3,
                      epilogue=_shortconv_epilogue, outs=[_row_col_tile(t, d, BF16, tn)], tn=tn,
                      extras=[_col_vec(conv_w.astype(F32), tn)], halo=True, lag=True, seq_len=t // batch)
    return matmul_residual(y, w_out, layer, x, next_gain, name="sc_out_proj", scale=1.0)


def _rope_table_kernel(pos_ref, freq_ref, cos_ref, sin_ref):
    ang = pos_ref[0].astype(F32) * freq_ref[...]
    lane = lax.broadcasted_iota(jnp.int32, ang.shape, 1)
    sign = jnp.where(lane < HEAD_DIM // 2, -1.0, 1.0)
    cos_ref[0] = jnp.cos(ang)
    sin_ref[0] = jnp.sin(ang) * sign


def rope_tables(positions, *, ts):
    b, s = positions.shape
    ts = _tile(s, ts)
    half = HEAD_DIM // 2
    inv_freq = ROPE_THETA ** (-jnp.arange(half, dtype=F32) / half)
    freq = jnp.concatenate([inv_freq, inv_freq]).reshape(1, HEAD_DIM)
    blk = pl.BlockSpec((1, ts, HEAD_DIM), lambda bi, si: (bi, si, 0))
    return pl.pallas_call(
        _rope_table_kernel,
        grid=(b, s // ts),
        in_specs=[pl.BlockSpec((1, ts, 1), lambda bi, si: (bi, si, 0)),
                  pl.BlockSpec((1, HEAD_DIM), lambda bi, si: (0, 0))],
        out_specs=[blk, blk],
        out_shape=[jax.ShapeDtypeStruct((b, s, HEAD_DIM), F32)] * 2,
        compiler_params=_params("parallel", "parallel"),
        name="rope_tables",
    )(positions.reshape(b, s, 1), freq)


def _norm_rope(seg, cos, sin, gain):
    seg = seg * lax.rsqrt(jnp.mean(seg * seg, axis=-1, keepdims=True) + EPS) * gain
    return seg * cos + pltpu.roll(seg, HEAD_DIM // 2, axis=1) * sin


def _moba_q_epilogue(accs, extra_refs, out_refs, cols):
    cos_ref, sin_ref, gain_ref = extra_refs
    cos, sin, gain = cos_ref[...], sin_ref[...], gain_ref[...]
    acc = accs[0]
    for _, hl in _heads(cols):
        seg = acc[:, hl.start - cols.start:hl.stop - cols.start]
        qh = _norm_rope(seg, cos, sin, gain) * (HEAD_DIM ** -0.5 * LOG2_E)
        out_refs[0][:, hl] = qh.astype(out_refs[0].dtype)


def _moba_k_epilogue(accs, extra_refs, out_refs, cols):
    cos_ref, sin_ref, gain_ref = extra_refs
    cos, sin, gain = cos_ref[...], sin_ref[...], gain_ref[...]
    ko_ref, km_ref = out_refs
    acc = accs[0]
    bs = MOBA_BLOCK
    for h, hl in _heads(cols):
        kh = _norm_rope(acc[:, hl.start - cols.start:hl.stop - cols.start], cos, sin, gain)
        for g in range(acc.shape[0] // bs):
            blk = kh[g * bs:(g + 1) * bs]
            ko_ref[0, h, g] = blk.astype(ko_ref.dtype)
            km_ref[g, :, hl] = jnp.mean(blk, axis=0, keepdims=True)


def _moba_v_epilogue(accs, extra_refs, out_refs, cols):
    vto_ref = out_refs[0]
    acc = accs[0]
    bs, dh = MOBA_BLOCK, HEAD_DIM
    sub = lax.broadcasted_iota(jnp.int32, (SUM_ROWS, bs), 0)
    ones_rows = jnp.where(sub == 0, 1.0, 0.0).astype(vto_ref.dtype)
    for h, hl in _heads(cols):
        for g in range(acc.shape[0] // bs):
            blk = acc[g * bs:(g + 1) * bs, hl.start - cols.start:hl.stop - cols.start]
            vto_ref[0, h, g, 0:dh, :] = blk.T.astype(vto_ref.dtype)
            vto_ref[0, h, g, dh:dh + SUM_ROWS, :] = ones_rows


def _moba_attn_kernel(q_ref, k_ref, vt_ref, km_ref, o_ref, bias_ref, *, group, slab):
    bs, dh = MOBA_BLOCK, HEAD_DIM
    nb = km_ref.shape[2]
    qi = pl.program_id(2)
    q = q_ref[0]

    gate = None
    for part in _split3(km_ref[0, 0]):
        term = _dot_nt(part, q)
        gate = term if gate is None else gate + term
    blk = lax.broadcasted_iota(jnp.int32, (nb, bs), 0)
    past = blk < qi
    gate = jnp.where(past, gate, -jnp.inf)
    blk_f = blk.astype(F32)
    bias = jnp.full((nb, bs), -jnp.inf, F32)
    for _ in range(MOBA_TOPK):
        top = jnp.max(gate, axis=0, keepdims=True)
        first = jnp.min(jnp.where(gate == top, blk_f, float(nb)), axis=0, keepdims=True)
        hit = (blk_f == first) & (top > -jnp.inf)
        bias = jnp.where(hit, 0.0, bias)
        gate = jnp.where(hit, -jnp.inf, gate)
    bias_ref[...] = bias

    def scores(j):
        return _dot_nt(k_ref[0, 0, j], q)

    def slab_scores(j):
        return _dot_nt(k_ref[0, 0, pl.ds(j, slab)].reshape(slab * bs, dh), q)

    def pieces(s_t):
        m_blk = jnp.max(s_t, axis=0, keepdims=True)
        return m_blk, jnp.exp2(s_t - m_blk).astype(BF16)

    def values(j, p):
        return jnp.dot(vt_ref[0, 0, j], p, preferred_element_type=F32)[:dh + 8]

    def merge(m, acc, parts):
        m_new = functools.reduce(jnp.maximum, [m] + [mb for mb, _ in parts])
        acc = jnp.exp2(m - m_new) * acc
        for mb, ab in parts:
            acc = acc + jnp.exp2(mb - m_new) * ab
        return m_new, acc

    key_pos = lax.broadcasted_iota(jnp.int32, (bs, bs), 0)
    qry_pos = lax.broadcasted_iota(jnp.int32, (bs, bs), 1)
    m, p = pieces(jnp.where(key_pos <= qry_pos, scores(qi), -jnp.inf))
    acc = values(qi, p)

    def body(it, carry):
        m, acc = carry
        j0 = it * group
        slabs = [slab_scores(j0 + u * slab) for u in range(group // slab)]
        mp = [pieces(slabs[b // slab][(b % slab) * bs:(b % slab + 1) * bs]) for b in range(group)]
        parts = [(mp[b][0] + bias_ref[pl.ds(j0 + b, 1), :], values(j0 + b, mp[b][1]))
                 for b in range(group)]
        return merge(m, acc, parts)

    m, acc = lax.fori_loop(0, (qi + group - 1) // group, body, (m, acc))
    o_ref[0] = (acc[:dh] / acc[dh:dh + 1]).T.astype(o_ref.dtype)


def moba_attn(q, k, vt, kmean):
    b, s, d = q.shape
    _, n_heads, nb, bs, dh = k.shape
    group = _tile(nb, 8)
    slab = _tile(group, 2)
    return pl.pallas_call(
        functools.partial(_moba_attn_kernel, group=group, slab=slab),
        grid=(b, n_heads, nb),
        in_specs=[pl.BlockSpec((1, bs, dh), lambda bi, hi, qi: (bi, qi, hi)),
                  pl.BlockSpec((1, 1, nb, bs, dh), lambda bi, hi, qi: (bi, hi, 0, 0, 0)),
                  pl.BlockSpec((1, 1, nb, dh + SUM_ROWS, bs), lambda bi, hi, qi: (bi, hi, 0, 0, 0)),
                  pl.BlockSpec((1, 1, nb, dh), lambda bi, hi, qi: (bi, hi, 0, 0))],
        out_specs=pl.BlockSpec((1, bs, dh), lambda bi, hi, qi: (bi, qi, hi)),
        out_shape=jax.ShapeDtypeStruct((b, s, d), BF16),
        scratch_shapes=[pltpu.VMEM((nb, bs), F32)],
        compiler_params=_params("parallel", "parallel", "arbitrary"),
        name="moba_attn",
    )(q, k, vt, kmean)


def moba_attention(x, xb, ss, positions, w_in, layer, q_norm, k_norm, w_out, next_gain):
    t, d = x.shape
    batch, s = positions.shape
    n_heads = d // HEAD_DIM
    dh, bs = HEAD_DIM, MOBA_BLOCK
    assert s % bs == 0
    nb = s // bs
    tm = _tile(t, ROW_TILE)
    assert tm % bs == 0 and s % tm == 0
    tn = _tile(d, 512)
    hpt, bpt, tiles_per_seq = tn // dh, tm // bs, s // tm
    cos, sin = rope_tables(positions, ts=512)
    rows = pl.BlockSpec((tm, dh), lambda i, j: (i, 0))
    vec = pl.BlockSpec((1, dh), lambda i, j: (0, 0))
    rope_extras = lambda gain: [(cos.reshape(t, dh), rows), (sin.reshape(t, dh), rows),
                                (gain.reshape(1, dh).astype(F32), vec)]
    blocked = lambda i, j: (i // tiles_per_seq, j, i % tiles_per_seq, 0, 0)
    (q,) = projection(xb, ss, w_in, layer, name="moba_q_proj", col0=0, n_cols=d, n_streams=1,
                      epilogue=_moba_q_epilogue, outs=[_row_col_tile(t, d, BF16, tn)], tn=tn,
                      extras=rope_extras(q_norm), lag=True)
    k, kmean = projection(
        xb, ss, w_in, layer, name="moba_k_proj", col0=d, n_cols=d, n_streams=1,
        epilogue=_moba_k_epilogue, tn=tn, extras=rope_extras(k_norm), lag=True,
        outs=[(jax.ShapeDtypeStruct((batch, n_heads, nb, bs, dh), BF16),
               pl.BlockSpec((1, hpt, bpt, bs, dh), blocked)),
              (jax.ShapeDtypeStruct((t // bs, 1, d), F32),
               pl.BlockSpec((bpt, 1, tn), lambda i, j: (i, 0, j)))])
    (vt,) = projection(
        xb, ss, w_in, layer, name="moba_v_proj", col0=2 * d, n_cols=d, n_streams=1,
        epilogue=_moba_v_epilogue, tn=tn,
        outs=[(jax.ShapeDtypeStruct((batch, n_heads, nb, dh + SUM_ROWS, bs), BF16),
               pl.BlockSpec((1, hpt, bpt, dh + SUM_ROWS, bs), blocked))])
    kmean = kmean.reshape(batch, nb, n_heads, dh).transpose(0, 2, 1, 3)
    o = moba_attn(q.reshape(batch, s, d), k, vt, kmean)
    return matmul_residual(o.reshape(t, d), w_out, layer, x, next_gain, name="moba_out_proj", scale=1.0)


def swiglu_half_step(x, xb, ss, w_in, w_out, layer, next_gain):
    t = x.shape[0]
    f = w_in.shape[2] // 2
    tn = 512
    (act,) = projection(xb, ss, w_in, layer, name="ffn_in", col0=0, n_cols=f, n_streams=2,
                        epilogue=_swiglu_epilogue, outs=[_row_col_tile(t, f, BF16, tn)], tn=tn)
    return matmul_residual(act, w_out, layer, x, next_gain, name="ffn_out", scale=0.5)


def kernel(x, positions, norm_ffn1, norm_mix, norm_ffn2, ffn1_w_in, ffn1_w_out, ffn2_w_in, ffn2_w_out, gdn_w_in, gdn_conv_w, gdn_a_log, gdn_dt_bias, gdn_out_norm, gdn_w_out, sc_w_in, sc_conv_w, sc_w_out, moba_w_in, moba_q_norm, moba_k_norm, moba_w_out):
    batch, s, d = x.shape
    depth = norm_ffn1.shape[0]
    ffn1_w_in, ffn1_w_out, ffn2_w_in, ffn2_w_out, gdn_w_out, sc_w_in, sc_w_out, moba_w_in, moba_w_out = (
        w.astype(BF16) for w in (ffn1_w_in, ffn1_w_out, ffn2_w_in, ffn2_w_out, gdn_w_out, sc_w_in,
                                 sc_w_out, moba_w_in, moba_w_out))
    gdn_w_big, gdn_w_small = gdn_split_weights(gdn_w_in, gdn_a_log.shape[1])
    xt = x.reshape(batch * s, d)
    xb, ss = scale_rows(xt, norm_ffn1[0], tm=256)
    for i in range(depth):
        kind, j = i % 3, i // 3
        after_layer = norm_ffn1[i + 1] if i + 1 < depth else None
        xt, xb, ss = swiglu_half_step(xt, xb, ss, ffn1_w_in, ffn1_w_out, i, norm_mix[i])
        if kind == 0:
            xt, xb, ss = gated_deltanet(xt, xb, ss, gdn_w_big, gdn_w_small, j, gdn_conv_w[j], gdn_a_log[j],
                                        gdn_dt_bias[j], gdn_out_norm[j], gdn_w_out, norm_ffn2[i], batch)
        elif kind == 1:
            xt, xb, ss = short_conv(xt, xb, ss, sc_w_in, j, sc_conv_w[j], sc_w_out, norm_ffn2[i], batch)
        else:
            xt, xb, ss = moba_attention(xt, xb, ss, positions, moba_w_in, j, moba_q_norm[j],
                                        moba_k_norm[j], moba_w_out, norm_ffn2[i])
        xt, xb, ss = swiglu_half_step(xt, xb, ss, ffn2_w_in, ffn2_w_out, i, after_layer)
    return xt.reshape(batch, s, d)
```

```python
import functools

import jax
import jax.numpy as jnp
from jax import lax
from jax.experimental import pallas as pl
from jax.experimental.pallas import tpu as pltpu

F32 = jnp.float32
BF16 = jnp.bfloat16

EPS = 1e-6
HEAD_DIM = 128
GDN_CHUNK = 64
MOBA_BLOCK = 256
MOBA_TOPK = 3
ROPE_THETA = 10000.0
LOG2_E = 1.4426950408889634
SUM_ROWS = 16

LANES = 128
MXU_COLS = 256
HALO_ROWS = 16
VMEM_LIMIT = 56 * 1024 * 1024
ROW_TILE = 1024
RES_ROW_TILE = 512
RES_COL_TILE = 1024
CONV_ROW_CHUNK = 128


def _tile(dim, want):
    t = min(dim, want)
    while dim % t:
        t //= 2
    return t


def _params(*sem):
    return pltpu.CompilerParams(dimension_semantics=sem, vmem_limit_bytes=VMEM_LIMIT)


def _dot(a, b):
    return jnp.dot(a.astype(BF16), b.astype(BF16), preferred_element_type=F32)


def _dot_nt(a, b):
    return lax.dot_general(a.astype(BF16), b.astype(BF16), (((1,), (1,)), ((), ())),
                           preferred_element_type=F32)


def _dot_tn(a, b):
    return lax.dot_general(a.astype(BF16), b.astype(BF16), (((0,), (0,)), ((), ())),
                           preferred_element_type=F32)


def _split3(x):
    hi = x.astype(BF16)
    r1 = x - hi.astype(F32)
    mid = r1.astype(BF16)
    lo = (r1 - mid.astype(F32)).astype(BF16)
    return hi, mid, lo


def _silu(x):
    return x * jax.nn.sigmoid(x)


def _fold_lanes(x):
    parts = [x[:, c * LANES:(c + 1) * LANES] for c in range(x.shape[1] // LANES)]
    return functools.reduce(jnp.add, parts)


def _scale_rows_kernel(x_ref, g_ref, xb_ref, ss_ref):
    x = x_ref[...]
    xb_ref[...] = (x * g_ref[...]).astype(BF16)
    ss_ref[...] = _fold_lanes(x * x)


def scale_rows(x, gain, *, tm):
    t, d = x.shape
    tm = _tile(t, tm)
    return pl.pallas_call(
        _scale_rows_kernel,
        grid=(t // tm,),
        in_specs=[pl.BlockSpec((tm, d), lambda i: (i, 0)), pl.BlockSpec((1, d), lambda i: (0, 0))],
        out_specs=[pl.BlockSpec((tm, d), lambda i: (i, 0)), pl.BlockSpec((tm, LANES), lambda i: (i, 0))],
        out_shape=[jax.ShapeDtypeStruct((t, d), BF16), jax.ShapeDtypeStruct((t, LANES), F32)],
        compiler_params=_params("parallel"),
        name="scale_rows",
    )(x, gain.reshape(1, d).astype(F32))


def _mm_residual_kernel(*refs, scale, emit_next):
    if emit_next:
        a_ref, w_ref, x_ref, g_ref, o_ref, xb_ref, ss_ref = refs
    else:
        a_ref, w_ref, x_ref, o_ref = refs
    xn = x_ref[...] + scale * jnp.dot(a_ref[...], w_ref[...], preferred_element_type=F32)
    o_ref[...] = xn
    if emit_next:
        xb_ref[...] = (xn * g_ref[...]).astype(BF16)
        part = _fold_lanes(xn * xn)

        @pl.when(pl.program_id(1) == 0)
        def _():
            ss_ref[...] = part

        @pl.when(pl.program_id(1) != 0)
        def _():
            ss_ref[...] += part


def matmul_residual(a, w, layer, x, next_gain, *, name, scale):
    t, k = a.shape
    n = w.shape[2]
    tm, tn = _tile(t, RES_ROW_TILE), _tile(n, RES_COL_TILE)
    emit_next = next_gain is not None
    tile = pl.BlockSpec((tm, tn), lambda i, j: (i, j))
    in_specs = [pl.BlockSpec((tm, k), lambda i, j: (i, 0)),
                pl.BlockSpec((pl.Squeezed(), k, tn), lambda i, j: (layer, 0, j)),
                tile]
    args = [a, w, x]
    out_specs, out_shape = [tile], [jax.ShapeDtypeStruct((t, n), F32)]
    if emit_next:
        in_specs.append(pl.BlockSpec((1, tn), lambda i, j: (0, j)))
        args.append(next_gain.reshape(1, n).astype(F32))
        out_specs += [tile, pl.BlockSpec((tm, LANES), lambda i, j: (i, 0))]
        out_shape += [jax.ShapeDtypeStruct((t, n), BF16), jax.ShapeDtypeStruct((t, LANES), F32)]
    outs = pl.pallas_call(
        functools.partial(_mm_residual_kernel, scale=scale, emit_next=emit_next),
        grid=(t // tm, n // tn),
        in_specs=in_specs,
        out_specs=out_specs,
        out_shape=out_shape,
        compiler_params=_params("parallel", "arbitrary"),
        name=name,
    )(*args)
    return tuple(outs) if emit_next else (outs[0], None, None)


def _proj_kernel(*refs, n_streams, n_extra, n_out, epilogue, halo, lag, seq_len, k_dim, nj, n_tiles):
    xb_ref, ss_ref = refs[0], refs[1]
    pos = 2
    if halo:
        xh_ref, sh_ref = refs[2], refs[3]
        pos = 4
    w_refs = refs[pos:pos + n_streams]
    pos += n_streams
    extra_refs = refs[pos:pos + n_extra]
    pos += n_extra
    out_refs = refs[pos:pos + n_out]
    pos += n_out
    scratch = list(refs[pos:])
    r_ref = scratch.pop(0)
    a_ref = scratch.pop(0) if halo else None
    tm = xb_ref.shape[0]
    tn = w_refs[0].shape[1]
    hr = HALO_ROWS if halo else 0
    step = pl.program_id(0)
    tile = jnp.minimum(step, n_tiles - 1)

    def row_factor(ss):
        return lax.rsqrt(jnp.sum(ss, axis=-1, keepdims=True) * (1.0 / k_dim) + EPS)

    seq_start = ((tile // nj) * tm) % seq_len == 0 if halo else None

    @pl.when(tile % nj == 0)
    def _():
        r_ref[hr:hr + tm, :] = row_factor(ss_ref[...])
        if halo:
            a_ref[hr:hr + tm, :] = xb_ref[...]
            a_ref[0:hr, :] = xh_ref[...]
            r_ref[0:hr, :] = jnp.where(seq_start, 0.0, row_factor(sh_ref[...]))

    def products(cols):
        a = a_ref[...] if halo else xb_ref[...]
        r = r_ref[...]
        return [jnp.dot(a, w[:, cols], preferred_element_type=F32) * r for w in w_refs]

    if not lag:
        strip = min(tn, MXU_COLS)
        for c0 in range(0, tn, strip):
            cols = slice(c0, c0 + strip)
            epilogue(products(cols), extra_refs, out_refs, cols)
        return

    slots = [scratch[:n_streams], scratch[n_streams:]]
    everything = slice(0, tn)

    @pl.when(step == 0)
    def _():
        for acc_ref in slots[1]:
            acc_ref[...] = jnp.zeros(acc_ref.shape, F32)

    def phase(cur, prev):
        epilogue(slots[prev], extra_refs, out_refs, everything)
        for acc_ref, acc in zip(slots[cur], products(everything)):
            acc_ref[...] = acc

    @pl.when(step % 2 == 0)
    def _():
        phase(0, 1)

    @pl.when(step % 2 == 1)
    def _():
        phase(1, 0)


def projection(xb, ss, w, layer, *, name, col0, n_cols, n_streams, epilogue, outs, tn,
               extras=(), halo=False, lag=False, seq_len=None):
    t, k = xb.shape
    tm, tn = _tile(t, ROW_TILE), _tile(n_cols, tn)
    assert col0 % tn == 0
    nj, j0 = n_cols // tn, col0 // tn
    n_tiles = (t // tm) * nj
    hr = HALO_ROWS if halo else 0

    def cur(step):
        tile = jnp.minimum(step, n_tiles - 1)
        return tile // nj, tile % nj

    def out_tile(step):
        tile = jnp.maximum(step - 1, 0) if lag else step
        return tile // nj, tile % nj

    on_cur = lambda f: (lambda step: f(*cur(step)))
    on_out = lambda spec: pl.BlockSpec(spec.block_shape, lambda step, f=spec.index_map: f(*out_tile(step)))
    rows_mode = pl.Buffered(1) if (halo or tn >= 1024) else None
    in_specs = [pl.BlockSpec((tm, k), on_cur(lambda i, j: (i, 0)), pipeline_mode=rows_mode),
                pl.BlockSpec((tm, LANES), on_cur(lambda i, j: (i, 0)))]
    args = [xb, ss]
    scratch = [pltpu.VMEM((tm + hr, 1), F32)]
    if halo:
        assert seq_len % tm == 0 and tm % HALO_ROWS == 0
        per = tm // HALO_ROWS
        prev = on_cur(lambda i, j: (jnp.maximum(i * per - 1, 0), 0))
        in_specs += [pl.BlockSpec((HALO_ROWS, k), prev), pl.BlockSpec((HALO_ROWS, LANES), prev)]
        args += [xb, ss]
        scratch.append(pltpu.VMEM((tm + HALO_ROWS, k), BF16))
    if lag:
        scratch += [pltpu.VMEM((tm + hr, tn), F32)] * (2 * n_streams)
    for s in range(n_streams):
        in_specs.append(pl.BlockSpec((pl.Squeezed(), k, tn),
                                     on_cur(lambda i, j, s=s: (layer, 0, j0 + j + s * nj))))
        args.append(w)
    for arr, spec in extras:
        in_specs.append(on_out(spec))
        args.append(arr)
    body = functools.partial(_proj_kernel, n_streams=n_streams, n_extra=len(extras), n_out=len(outs),
                             epilogue=epilogue, halo=halo, lag=lag, seq_len=seq_len, k_dim=k, nj=nj,
                             n_tiles=n_tiles)
    return pl.pallas_call(
        body,
        grid=(n_tiles + int(lag),),
        in_specs=in_specs,
        out_specs=[on_out(spec) for _, spec in outs],
        out_shape=[shape for shape, _ in outs],
        scratch_shapes=scratch,
        compiler_params=_params("arbitrary"),
        name=name,
    )(*args)


def _row_col_tile(t, n, dtype, tn):
    tm, tn = _tile(t, ROW_TILE), _tile(n, tn)
    return jax.ShapeDtypeStruct((t, n), dtype), pl.BlockSpec((tm, tn), lambda i, j: (i, j))


def _col_vec(v, tn):
    return v, pl.BlockSpec((v.shape[0], tn), lambda i, j: (0, j))


def _conv_row_chunks(exts, w_ref, cols, row_chunk=CONV_ROW_CHUNK):
    width = w_ref.shape[0]
    tm = exts[0].shape[0] - HALO_ROWS
    chunk = _tile(tm, row_chunk)
    for r0 in range(0, tm, chunk):
        acc = None
        for tap in range(width):
            start = HALO_ROWS - (width - 1) + tap + r0
            term = functools.reduce(jnp.multiply, [e[start:start + chunk, :] for e in exts])
            term = term * w_ref[tap:tap + 1, cols]
            acc = term if acc is None else acc + term
        yield slice(r0, r0 + chunk), acc


def _heads(cols):
    return [(c // HEAD_DIM, slice(c, c + HEAD_DIM)) for c in range(cols.start, cols.stop, HEAD_DIM)]


def _swiglu_epilogue(accs, extra_refs, out_refs, cols):
    gate, up = accs
    out_refs[0][:, cols] = (_silu(gate) * up).astype(out_refs[0].dtype)


def _plain_epilogue(accs, extra_refs, out_refs, cols):
    out_refs[0][:, cols] = accs[0].astype(out_refs[0].dtype)


def _gdn_qk_epilogue(accs, extra_refs, out_refs, cols):
    conv_ref, scale_ref = extra_refs
    for rows, conv in _conv_row_chunks(accs, conv_ref, cols):
        y = _silu(conv)
        for _, hl in _heads(cols):
            seg = y[:, hl.start - cols.start:hl.stop - cols.start]
            seg = seg * lax.rsqrt(jnp.sum(seg * seg, axis=-1, keepdims=True) + EPS) * scale_ref[:, hl]
            out_refs[0][rows, hl] = seg.astype(out_refs[0].dtype)


def _gdn_v_epilogue(accs, extra_refs, out_refs, cols):
    for rows, conv in _conv_row_chunks(accs, extra_refs[0], cols):
        out_refs[0][rows, cols] = _silu(conv).astype(out_refs[0].dtype)


def _gdn_gates_kernel(a_ref, bt_ref, alog_ref, dtb_ref, gc_ref, beta_ref):
    c = GDN_CHUNK
    xs = a_ref[...] + dtb_ref[...]
    softplus = jnp.maximum(xs, 0.0) + jnp.log(1.0 + jnp.exp(-jnp.abs(xs)))
    g = -jnp.exp(alog_ref[...]) * softplus
    row = lax.broadcasted_iota(jnp.int32, (c, c), 0)
    col = lax.broadcasted_iota(jnp.int32, (c, c), 1)
    tri = (row >= col).astype(BF16)
    for r in range(0, g.shape[0], c):
        hi, mid, lo = _split3(g[r:r + c, :])
        gc_ref[r:r + c, :] = (jnp.dot(tri, hi, preferred_element_type=F32)
                              + jnp.dot(tri, mid, preferred_element_type=F32)
                              + jnp.dot(tri, lo, preferred_element_type=F32))
    beta_ref[...] = jax.nn.sigmoid(bt_ref[...])


def gdn_gates(ab, a_log, dt_bias, *, tb):
    t = ab.shape[0]
    h = a_log.shape[0]
    tb = _tile(t, tb)
    pad = lambda v: jnp.pad(v.astype(F32), (0, LANES - h)).reshape(1, LANES)
    row_blk = lambda j: pl.BlockSpec((tb, LANES), lambda i, j=j: (i, j))
    vec = pl.BlockSpec((1, LANES), lambda i: (0, 0))
    return pl.pallas_call(
        _gdn_gates_kernel,
        grid=(t // tb,),
        in_specs=[row_blk(0), row_blk(1), vec, vec],
        out_specs=[row_blk(0), row_blk(0)],
        out_shape=[jax.ShapeDtypeStruct((t, LANES), F32)] * 2,
        compiler_params=_params("parallel"),
        name="gdn_gates",
    )(ab, ab, pad(a_log), pad(dt_bias))


def _gdn_chunk_kernel(q_ref, k_ref, v_ref, z_ref, gcc_ref, bc_ref, gcr_ref, onorm_ref, o_ref,
                      state_ref, *, heads_per_step):
    c = GDN_CHUNK
    dk = HEAD_DIM
    ts = q_ref.shape[1]

    @pl.when(pl.program_id(2) == 0)
    def _():
        state_ref[...] = jnp.zeros(state_ref.shape, F32)

    row = lax.broadcasted_iota(jnp.int32, (c, c), 0)
    col = lax.broadcasted_iota(jnp.int32, (c, c), 1)
    incl = row >= col
    strict = row > col
    eye = (row == col).astype(F32)
    lane = lax.broadcasted_iota(jnp.int32, (ts, LANES), 1)
    hb, nc = heads_per_step, ts // c

    per_head = []
    for i in range(hb):
        head = pl.program_id(1) * hb + i
        pick = lane == head
        gcol = jnp.sum(jnp.where(pick, gcc_ref[0], 0.0), axis=-1, keepdims=True)
        bcol = jnp.sum(jnp.where(pick, bc_ref[0], 0.0), axis=-1, keepdims=True)
        grow = gcr_ref[0, pl.ds(head, 1), :]
        per_head.append((gcol, bcol, grow))
    items = [(ci, i) for ci in range(nc) for i in range(hb)]
    qs, ks, vs, gcols, bcols, grows = [], [], [], [], [], []
    for ci, i in items:
        gcol, bcol, grow = per_head[i]
        sl = slice(ci * c, (ci + 1) * c)
        hl = slice(i * dk, (i + 1) * dk)
        qs.append(q_ref[0, sl, hl])
        ks.append(k_ref[0, sl, hl])
        vs.append(v_ref[0, sl, hl].astype(F32))
        gcols.append(gcol[sl])
        bcols.append(bcol[sl])
        grows.append(grow[:, sl])
    n = len(items)
    rng = range(n)
    kfs = [ks[t].astype(F32) for t in rng]
    gammas = [jnp.where(incl, jnp.exp(jnp.where(incl, gcols[t] - grows[t], 0.0)), 0.0) for t in rng]
    kbs = [kfs[t] * bcols[t] for t in rng]
    n_pows = [jnp.where(strict, -(_dot_nt(kbs[t], ks[t]) * gammas[t]), 0.0) for t in rng]
    t_invs = [eye + n_pows[t] for t in rng]
    for _ in range(5):
        n_pows = [_dot(n_pows[t], n_pows[t]) for t in rng]
        t_invs = [t_invs[t] + _dot(t_invs[t], n_pows[t]) for t in rng]
    e_gs = [jnp.exp(gcols[t]) for t in rng]
    wus = [_dot(t_invs[t], jnp.concatenate([kbs[t] * e_gs[t], vs[t] * bcols[t]], axis=1)) for t in rng]
    qks = [jnp.where(incl, _dot_nt(qs[t], ks[t]) * gammas[t], 0.0) for t in rng]
    g_lasts = [grows[t][:, c - 1:c] for t in rng]
    q_decs = [qs[t].astype(F32) * e_gs[t] for t in rng]
    k_decs = [kfs[t] * jnp.exp(g_lasts[t] - gcols[t]) for t in rng]
    decays = [jnp.exp(g_lasts[t]) for t in rng]

    states = [state_ref[i] for i in range(hb)]
    outs = [None] * n
    for ci in range(nc):
        idx = [ci * hb + i for i in range(hb)]
        v_news = [wus[t][:, dk:] - _dot(wus[t][:, :dk], states[i]) for i, t in enumerate(idx)]
        o_state = [_dot(q_decs[t], states[i]) for i, t in enumerate(idx)]
        for i, t in enumerate(idx):
            outs[t] = o_state[i] + _dot(qks[t], v_news[i])
        states = [states[i] * decays[t] + _dot_tn(k_decs[t], v_news[i]) for i, t in enumerate(idx)]
    for i in range(hb):
        state_ref[i] = states[i]

    for t, (ci, i) in enumerate(items):
        o = outs[t]
        o = o * lax.rsqrt(jnp.mean(o * o, axis=-1, keepdims=True) + EPS) * onorm_ref[...]
        z = z_ref[0, ci * c:(ci + 1) * c, i * dk:(i + 1) * dk].astype(F32)
        o_ref[0, ci * c:(ci + 1) * c, i * dk:(i + 1) * dk] = (o * _silu(z)).astype(o_ref.dtype)


def gdn_chunk(qk, v, z, gcc, betac, gcr, o_norm, *, n_heads, ts, heads_per_step):
    b, s, _ = v.shape
    dk = HEAD_DIM
    ts = _tile(s, ts)
    hb = _tile(n_heads, heads_per_step)
    groups = n_heads // hb
    assert ts % GDN_CHUNK == 0
    tok = lambda part: pl.BlockSpec((1, ts, hb * dk), lambda bi, hi, si, p=part: (bi, si, hi + p * groups))
    gate_col = pl.BlockSpec((1, ts, LANES), lambda bi, hi, si: (bi, si, 0))
    return pl.pallas_call(
        functools.partial(_gdn_chunk_kernel, heads_per_step=hb),
        grid=(b, groups, s // ts),
        in_specs=[tok(0), tok(1), tok(0), tok(0), gate_col, gate_col,
                  pl.BlockSpec((1, n_heads, ts), lambda bi, hi, si: (bi, 0, si)),
                  pl.BlockSpec((1, dk), lambda bi, hi, si: (0, 0))],
        out_specs=tok(0),
        out_shape=jax.ShapeDtypeStruct((b, s, n_heads * dk), BF16),
        scratch_shapes=[pltpu.VMEM((hb, dk, dk), F32)],
        compiler_params=_params("parallel", "parallel", "arbitrary"),
        name="gdn_chunk",
    )(qk, qk, v, z, gcc, betac, gcr, o_norm.reshape(1, dk).astype(F32))


def gdn_split_weights(w_in, n_heads):
    wide = 4 * n_heads * HEAD_DIM
    pad = ((0, 0), (0, 0), (0, LANES - n_heads))
    w_a = jnp.pad(w_in[:, :, wide:wide + n_heads], pad)
    w_b = jnp.pad(w_in[:, :, wide + n_heads:], pad)
    return w_in[:, :, :wide].astype(BF16), jnp.concatenate([w_a, w_b], axis=2).astype(BF16)


def gated_deltanet(x, xb, ss, w_big, w_small, layer, conv_w, a_log, dt_bias, o_norm, w_out,
                   next_gain, batch):
    t, d = x.shape
    n_heads = a_log.shape[0]
    s = t // batch
    hd = n_heads * HEAD_DIM
    conv_w = conv_w.astype(F32)
    tn = _tile(hd, 512)
    qk_scale = jnp.concatenate([jnp.full((1, hd), HEAD_DIM ** -0.5, F32), jnp.ones((1, hd), F32)], axis=1)
    (qk,) = projection(xb, ss, w_big, layer, name="gdn_qk_proj", col0=0, n_cols=2 * hd, n_streams=1,
                       epilogue=_gdn_qk_epilogue, outs=[_row_col_tile(t, 2 * hd, BF16, tn)], tn=tn,
                       extras=[_col_vec(conv_w[:, :2 * hd], tn), _col_vec(qk_scale, tn)],
                       halo=True, lag=True, seq_len=s)
    (v,) = projection(xb, ss, w_big, layer, name="gdn_v_proj", col0=2 * hd, n_cols=hd, n_streams=1,
                      epilogue=_gdn_v_epilogue, outs=[_row_col_tile(t, hd, BF16, tn)], tn=tn,
                      extras=[_col_vec(conv_w[:, 2 * hd:], tn)], halo=True, lag=True, seq_len=s)
    (z,) = projection(xb, ss, w_big, layer, name="gdn_z_proj", col0=3 * hd, n_cols=hd, n_streams=1,
                      epilogue=_plain_epilogue, outs=[_row_col_tile(t, hd, BF16, tn)], tn=tn)
    (ab,) = projection(xb, ss, w_small, layer, name="gdn_ab_proj", col0=0, n_cols=2 * LANES,
                       n_streams=1, epilogue=_plain_epilogue,
                       outs=[_row_col_tile(t, 2 * LANES, F32, 2 * LANES)], tn=2 * LANES)
    gcc, betac = gdn_gates(ab, a_log, dt_bias, tb=512)
    gcc = gcc.reshape(batch, s, LANES)
    betac = betac.reshape(batch, s, LANES)
    gcr = jnp.swapaxes(gcc[:, :, :n_heads], 1, 2)
    o = gdn_chunk(qk.reshape(batch, s, 2 * hd), v.reshape(batch, s, hd), z.reshape(batch, s, hd),
                  gcc, betac, gcr, o_norm, n_heads=n_heads, ts=256, heads_per_step=4)
    return matmul_residual(o.reshape(t, hd), w_out, layer, x, next_gain, name="gdn_out_proj", scale=1.0)


def _shortconv_epilogue(accs, extra_refs, out_refs, cols):
    gb, gc, xv = accs
    for rows, conv in _conv_row_chunks([gc, xv], extra_refs[0], cols):
        gate = gb[HALO_ROWS + rows.start:HALO_ROWS + rows.stop, :]
        out_refs[0][rows, cols] = (gate * conv).astype(out_refs[0].dtype)


def short_conv(x, xb, ss, w_in, layer, conv_w, w_out, next_gain, batch):
    t, d = x.shape
    tn = 256
    (y,) = projection(xb, ss, w_in, layer, name="sc_in_proj", col0=0, n_cols=d, n_streams=3,
                      epilogue=_shortconv_epilogue, outs=[_row_col_tile(t, d, BF16, tn)], tn=tn,
                      extras=[_col_vec(conv_w.astype(F32), tn)], halo=True, lag=True, seq_len=t // batch)
    return matmul_residual(y, w_out, layer, x, next_gain, name="sc_out_proj", scale=1.0)


def _rope_table_kernel(pos_ref, freq_ref, cos_ref, sin_ref):
    ang = pos_ref[0].astype(F32) * freq_ref[...]
    lane = lax.broadcasted_iota(jnp.int32, ang.shape, 1)
    sign = jnp.where(lane < HEAD_DIM // 2, -1.0, 1.0)
    cos_ref[0] = jnp.cos(ang)
    sin_ref[0] = jnp.sin(ang) * sign


def rope_tables(positions, *, ts):
    b, s = positions.shape
    ts = _tile(s, ts)
    half = HEAD_DIM // 2
    inv_freq = ROPE_THETA ** (-jnp.arange(half, dtype=F32) / half)
    freq = jnp.concatenate([inv_freq, inv_freq]).reshape(1, HEAD_DIM)
    blk = pl.BlockSpec((1, ts, HEAD_DIM), lambda bi, si: (bi, si, 0))
    return pl.pallas_call(
        _rope_table_kernel,
        grid=(b, s // ts),
        in_specs=[pl.BlockSpec((1, ts, 1), lambda bi, si: (bi, si, 0)),
                  pl.BlockSpec((1, HEAD_DIM), lambda bi, si: (0, 0))],
        out_specs=[blk, blk],
        out_shape=[jax.ShapeDtypeStruct((b, s, HEAD_DIM), F32)] * 2,
        compiler_params=_params("parallel", "parallel"),
        name="rope_tables",
    )(positions.reshape(b, s, 1), freq)


def _norm_rope(seg, cos, sin, gain):
    seg = seg * lax.rsqrt(jnp.mean(seg * seg, axis=-1, keepdims=True) + EPS) * gain
    return seg * cos + pltpu.roll(seg, HEAD_DIM // 2, axis=1) * sin


def _moba_q_epilogue(accs, extra_refs, out_refs, cols):
    cos_ref, sin_ref, gain_ref = extra_refs
    cos, sin, gain = cos_ref[...], sin_ref[...], gain_ref[...]
    acc = accs[0]
    for _, hl in _heads(cols):
        seg = acc[:, hl.start - cols.start:hl.stop - cols.start]
        qh = _norm_rope(seg, cos, sin, gain) * (HEAD_DIM ** -0.5 * LOG2_E)
        out_refs[0][:, hl] = qh.astype(out_refs[0].dtype)


def _moba_k_epilogue(accs, extra_refs, out_refs, cols):
    cos_ref, sin_ref, gain_ref = extra_refs
    cos, sin, gain = cos_ref[...], sin_ref[...], gain_ref[...]
    ko_ref, km_ref = out_refs
    acc = accs[0]
    bs = MOBA_BLOCK
    for h, hl in _heads(cols):
        kh = _norm_rope(acc[:, hl.start - cols.start:hl.stop - cols.start], cos, sin, gain)
        for g in range(acc.shape[0] // bs):
            blk = kh[g * bs:(g + 1) * bs]
            ko_ref[0, h, g] = blk.astype(ko_ref.dtype)
            km_ref[g, :, hl] = jnp.mean(blk, axis=0, keepdims=True)


def _moba_v_epilogue(accs, extra_refs, out_refs, cols):
    vto_ref = out_refs[0]
    acc = accs[0]
    bs, dh = MOBA_BLOCK, HEAD_DIM
    sub = lax.broadcasted_iota(jnp.int32, (SUM_ROWS, bs), 0)
    ones_rows = jnp.where(sub == 0, 1.0, 0.0).astype(vto_ref.dtype)
    for h, hl in _heads(cols):
        for g in range(acc.shape[0] // bs):
            blk = acc[g * bs:(g + 1) * bs, hl.start - cols.start:hl.stop - cols.start]
            vto_ref[0, h, g, 0:dh, :] = blk.T.astype(vto_ref.dtype)
            vto_ref[0, h, g, dh:dh + SUM_ROWS, :] = ones_rows


def _moba_attn_kernel(q_ref, k_ref, vt_ref, km_ref, o_ref, bias_ref, *, group, slab, heads):
    bs, dh = MOBA_BLOCK, HEAD_DIM
    nb = km_ref.shape[2]
    qi = pl.program_id(2)
    hrange = range(heads)
    q = [q_ref[0, :, h * dh:(h + 1) * dh] for h in hrange]

    blk = lax.broadcasted_iota(jnp.int32, (nb, bs), 0)
    blk_f = blk.astype(F32)
    past = blk < qi
    for h in hrange:
        gate = functools.reduce(jnp.add, [_dot_nt(part, q[h]) for part in _split3(km_ref[0, h])])
        gate = jnp.where(past, gate, -jnp.inf)
        bias = jnp.full((nb, bs), -jnp.inf, F32)
        for _ in range(MOBA_TOPK):
            top = jnp.max(gate, axis=0, keepdims=True)
            first = jnp.min(jnp.where(gate == top, blk_f, float(nb)), axis=0, keepdims=True)
            hit = (blk_f == first) & (top > -jnp.inf)
            bias = jnp.where(hit, 0.0, bias)
            gate = jnp.where(hit, -jnp.inf, gate)
        bias_ref[h] = bias

    def slab_scores(h, j, n):
        return _dot_nt(k_ref[0, h, pl.ds(j, n)].reshape(n * bs, dh), q[h])

    def pieces(s_t):
        m_blk = jnp.max(s_t, axis=0, keepdims=True)
        return m_blk, jnp.exp2(s_t - m_blk).astype(BF16)

    def values(h, j, p):
        return jnp.dot(vt_ref[0, h, j], p, preferred_element_type=F32)[:dh + 8]

    def merge(m, acc, parts):
        m_new = functools.reduce(jnp.maximum, [m] + [mb for mb, _ in parts])
        acc = jnp.exp2(m - m_new) * acc
        for mb, ab in parts:
            acc = acc + jnp.exp2(mb - m_new) * ab
        return m_new, acc

    key_pos = lax.broadcasted_iota(jnp.int32, (bs, bs), 0)
    qry_pos = lax.broadcasted_iota(jnp.int32, (bs, bs), 1)
    own = [slab_scores(h, qi, 1) for h in hrange]
    own = [pieces(jnp.where(key_pos <= qry_pos, s_t, -jnp.inf)) for s_t in own]
    carry = tuple((own[h][0], values(h, qi, own[h][1])) for h in hrange)

    def body(it, carry):
        j0 = it * group
        slabs = [[slab_scores(h, j0 + u * slab, slab) for u in range(group // slab)] for h in hrange]
        mp = [[pieces(slabs[h][b // slab][(b % slab) * bs:(b % slab + 1) * bs]) for b in range(group)]
              for h in hrange]
        parts = [[(mp[h][b][0] + bias_ref[h, pl.ds(j0 + b, 1), :], values(h, j0 + b, mp[h][b][1]))
                  for b in range(group)] for h in hrange]
        return tuple(merge(*carry[h], parts[h]) for h in hrange)

    carry = lax.fori_loop(0, (qi + group - 1) // group, body, carry)
    for h in hrange:
        acc = carry[h][1]
        o_ref[0, :, h * dh:(h + 1) * dh] = (acc[:dh] / acc[dh:dh + 1]).T.astype(o_ref.dtype)


def moba_attn(q, k, vt, kmean):
    b, s, d = q.shape
    _, n_heads, nb, bs, dh = k.shape
    group = _tile(nb, 8)
    slab = _tile(group, 2)
    heads = _tile(n_heads, 4)
    return pl.pallas_call(
        functools.partial(_moba_attn_kernel, group=group, slab=slab, heads=heads),
        grid=(b, n_heads // heads, nb),
        in_specs=[pl.BlockSpec((1, bs, heads * dh), lambda bi, hi, qi: (bi, qi, hi)),
                  pl.BlockSpec((1, heads, nb, bs, dh), lambda bi, hi, qi: (bi, hi, 0, 0, 0)),
                  pl.BlockSpec((1, heads, nb, dh + SUM_ROWS, bs), lambda bi, hi, qi: (bi, hi, 0, 0, 0)),
                  pl.BlockSpec((1, heads, nb, dh), lambda bi, hi, qi: (bi, hi, 0, 0))],
        out_specs=pl.BlockSpec((1, bs, heads * dh), lambda bi, hi, qi: (bi, qi, hi)),
        out_shape=jax.ShapeDtypeStruct((b, s, d), BF16),
        scratch_shapes=[pltpu.VMEM((heads, nb, bs), F32)],
        compiler_params=_params("parallel", "parallel", "arbitrary"),
        name="moba_attn",
    )(q, k, vt, kmean)


def moba_attention(x, xb, ss, positions, w_in, layer, q_norm, k_norm, w_out, next_gain):
    t, d = x.shape
    batch, s = positions.shape
    n_heads = d // HEAD_DIM
    dh, bs = HEAD_DIM, MOBA_BLOCK
    assert s % bs == 0
    nb = s // bs
    tm = _tile(t, ROW_TILE)
    assert tm % bs == 0 and s % tm == 0
    tn = _tile(d, 512)
    hpt, bpt, tiles_per_seq = tn // dh, tm // bs, s // tm
    cos, sin = rope_tables(positions, ts=512)
    rows = pl.BlockSpec((tm, dh), lambda i, j: (i, 0))
    vec = pl.BlockSpec((1, dh), lambda i, j: (0, 0))
    rope_extras = lambda gain: [(cos.reshape(t, dh), rows), (sin.reshape(t, dh), rows),
                                (gain.reshape(1, dh).astype(F32), vec)]
    blocked = lambda i, j: (i // tiles_per_seq, j, i % tiles_per_seq, 0, 0)
    (q,) = projection(xb, ss, w_in, layer, name="moba_q_proj", col0=0, n_cols=d, n_streams=1,
                      epilogue=_moba_q_epilogue, outs=[_row_col_tile(t, d, BF16, tn)], tn=tn,
                      extras=rope_extras(q_norm), lag=True)
    k, kmean = projection(
        xb, ss, w_in, layer, name="moba_k_proj", col0=d, n_cols=d, n_streams=1,
        epilogue=_moba_k_epilogue, tn=tn, extras=rope_extras(k_norm), lag=True,
        outs=[(jax.ShapeDtypeStruct((batch, n_heads, nb, bs, dh), BF16),
               pl.BlockSpec((1, hpt, bpt, bs, dh), blocked)),
              (jax.ShapeDtypeStruct((t // bs, 1, d), F32),
               pl.BlockSpec((bpt, 1, tn), lambda i, j: (i, 0, j)))])
    (vt,) = projection(
        xb, ss, w_in, layer, name="moba_v_proj", col0=2 * d, n_cols=d, n_streams=1,
        epilogue=_moba_v_epilogue, tn=tn,
        outs=[(jax.ShapeDtypeStruct((batch, n_heads, nb, dh + SUM_ROWS, bs), BF16),
               pl.BlockSpec((1, hpt, bpt, dh + SUM_ROWS, bs), blocked))])
    kmean = kmean.reshape(batch, nb, n_heads, dh).transpose(0, 2, 1, 3)
    o = moba_attn(q.reshape(batch, s, d), k, vt, kmean)
    return matmul_residual(o.reshape(t, d), w_out, layer, x, next_gain, name="moba_out_proj", scale=1.0)


def swiglu_half_step(x, xb, ss, w_in, w_out, layer, next_gain):
    t = x.shape[0]
    f = w_in.shape[2] // 2
    tn = 512
    (act,) = projection(xb, ss, w_in, layer, name="ffn_in", col0=0, n_cols=f, n_streams=2,
                        epilogue=_swiglu_epilogue, outs=[_row_col_tile(t, f, BF16, tn)], tn=tn)
    return matmul_residual(act, w_out, layer, x, next_gain, name="ffn_out", scale=0.5)


def kernel(x, positions, norm_ffn1, norm_mix, norm_ffn2, ffn1_w_in, ffn1_w_out, ffn2_w_in, ffn2_w_out, gdn_w_in, gdn_conv_w, gdn_a_log, gdn_dt_bias, gdn_out_norm, gdn_w_out, sc_w_in, sc_conv_w, sc_w_out, moba_w_in, moba_q_norm, moba_k_norm, moba_w_out):
    batch, s, d = x.shape
    depth = norm_ffn1.shape[0]
    ffn1_w_in, ffn1_w_out, ffn2_w_in, ffn2_w_out, gdn_w_out, sc_w_in, sc_w_out, moba_w_in, moba_w_out = (
        w.astype(BF16) for w in (ffn1_w_in, ffn1_w_out, ffn2_w_in, ffn2_w_out, gdn_w_out, sc_w_in,
                                 sc_w_out, moba_w_in, moba_w_out))
    gdn_w_big, gdn_w_small = gdn_split_weights(gdn_w_in, gdn_a_log.shape[1])
    xt = x.reshape(batch * s, d)
    xb, ss = scale_rows(xt, norm_ffn1[0], tm=256)
    for i in range(depth):
        kind, j = i % 3, i // 3
        after_layer = norm_ffn1[i + 1] if i + 1 < depth else None
        xt, xb, ss = swiglu_half_step(xt, xb, ss, ffn1_w_in, ffn1_w_out, i, norm_mix[i])
        if kind == 0:
            xt, xb, ss = gated_deltanet(xt, xb, ss, gdn_w_big, gdn_w_small, j, gdn_conv_w[j], gdn_a_log[j],
                                        gdn_dt_bias[j], gdn_out_norm[j], gdn_w_out, norm_ffn2[i], batch)
        elif kind == 1:
            xt, xb, ss = short_conv(xt, xb, ss, sc_w_in, j, sc_conv_w[j], sc_w_out, norm_ffn2[i], batch)
        else:
            xt, xb, ss = moba_attention(xt, xb, ss, positions, moba_w_in, j, moba_q_norm[j],
                                        moba_k_norm[j], moba_w_out, norm_ffn2[i])
        xt, xb, ss = swiglu_half_step(xt, xb, ss, ffn2_w_in, ffn2_w_out, i, after_layer)
    return xt.reshape(batch, s, d)
```

```python
import functools

import jax
import jax.numpy as jnp
from jax import lax
from jax.experimental import pallas as pl
from jax.experimental.pallas import tpu as pltpu

F32 = jnp.float32
BF16 = jnp.bfloat16

EPS = 1e-6
HEAD_DIM = 128
GDN_CHUNK = 64
MOBA_BLOCK = 256
MOBA_TOPK = 3
ROPE_THETA = 10000.0
LOG2_E = 1.4426950408889634
SUM_ROWS = 16

LANES = 128
MXU_COLS = 256
CONV_HALO = 8
HALO_ROWS = 16
VMEM_LIMIT = 56 * 1024 * 1024
ROW_TILE = 1024
RES_ROW_TILE = 1024
RES_COL_TILE = 512
CONV_ROW_CHUNK = 128


def _tile(dim, want):
    t = min(dim, want)
    while dim % t:
        t //= 2
    return t


def _params(*sem):
    return pltpu.CompilerParams(dimension_semantics=sem, vmem_limit_bytes=VMEM_LIMIT)


def _dot(a, b):
    return jnp.dot(a.astype(BF16), b.astype(BF16), preferred_element_type=F32)


def _dot_nt(a, b):
    return lax.dot_general(a.astype(BF16), b.astype(BF16), (((1,), (1,)), ((), ())),
                           preferred_element_type=F32)


def _dot_tn(a, b):
    return lax.dot_general(a.astype(BF16), b.astype(BF16), (((0,), (0,)), ((), ())),
                           preferred_element_type=F32)


def _split3(x):
    hi = x.astype(BF16)
    r1 = x - hi.astype(F32)
    mid = r1.astype(BF16)
    lo = (r1 - mid.astype(F32)).astype(BF16)
    return hi, mid, lo


def _silu(x):
    return x * jax.nn.sigmoid(x)


def _fold_lanes(x):
    parts = [x[:, c * LANES:(c + 1) * LANES] for c in range(x.shape[1] // LANES)]
    return functools.reduce(jnp.add, parts)


def _scale_rows_kernel(x_ref, g_ref, xb_ref, ss_ref):
    x = x_ref[...]
    xb_ref[...] = (x * g_ref[...]).astype(BF16)
    ss_ref[...] = _fold_lanes(x * x)


def scale_rows(x, gain, *, tm):
    t, d = x.shape
    tm = _tile(t, tm)
    return pl.pallas_call(
        _scale_rows_kernel,
        grid=(t // tm,),
        in_specs=[pl.BlockSpec((tm, d), lambda i: (i, 0)), pl.BlockSpec((1, d), lambda i: (0, 0))],
        out_specs=[pl.BlockSpec((tm, d), lambda i: (i, 0)), pl.BlockSpec((tm, LANES), lambda i: (i, 0))],
        out_shape=[jax.ShapeDtypeStruct((t, d), BF16), jax.ShapeDtypeStruct((t, LANES), F32)],
        compiler_params=_params("parallel"),
        name="scale_rows",
    )(x, gain.reshape(1, d).astype(F32))


def _mm_residual_kernel(*refs, scale, emit_next):
    if emit_next:
        a_ref, w_ref, x_ref, g_ref, o_ref, xb_ref, ss_ref = refs
    else:
        a_ref, w_ref, x_ref, o_ref = refs
    xn = x_ref[...] + scale * jnp.dot(a_ref[...], w_ref[...], preferred_element_type=F32)
    o_ref[...] = xn
    if emit_next:
        xb_ref[...] = (xn * g_ref[...]).astype(BF16)
        part = _fold_lanes(xn * xn)

        @pl.when(pl.program_id(1) == 0)
        def _():
            ss_ref[...] = part

        @pl.when(pl.program_id(1) != 0)
        def _():
            ss_ref[...] += part


def matmul_residual(a, w, layer, x, next_gain, *, name, scale):
    t, k = a.shape
    n = w.shape[2]
    tm, tn = _tile(t, RES_ROW_TILE), _tile(n, RES_COL_TILE)
    emit_next = next_gain is not None
    tile = pl.BlockSpec((tm, tn), lambda i, j: (i, j))
    in_specs = [pl.BlockSpec((tm, k), lambda i, j: (i, 0)),
                pl.BlockSpec((pl.Squeezed(), k, tn), lambda i, j: (layer, 0, j)),
                tile]
    args = [a, w, x]
    out_specs, out_shape = [tile], [jax.ShapeDtypeStruct((t, n), F32)]
    if emit_next:
        in_specs.append(pl.BlockSpec((1, tn), lambda i, j: (0, j)))
        args.append(next_gain.reshape(1, n).astype(F32))
        out_specs += [tile, pl.BlockSpec((tm, LANES), lambda i, j: (i, 0))]
        out_shape += [jax.ShapeDtypeStruct((t, n), BF16), jax.ShapeDtypeStruct((t, LANES), F32)]
    outs = pl.pallas_call(
        functools.partial(_mm_residual_kernel, scale=scale, emit_next=emit_next),
        grid=(t // tm, n // tn),
        in_specs=in_specs,
        out_specs=out_specs,
        out_shape=out_shape,
        compiler_params=_params("parallel", "arbitrary"),
        name=name,
    )(*args)
    return tuple(outs) if emit_next else (outs[0], None, None)


def _proj_kernel(*refs, n_streams, n_extra, n_out, epilogue, halo, lag, seq_len, k_dim, nj, n_tiles):
    xb_ref, ss_ref = refs[0], refs[1]
    pos = 2
    if halo:
        xh_ref, sh_ref = refs[2], refs[3]
        pos = 4
    w_refs = refs[pos:pos + n_streams]
    pos += n_streams
    extra_refs = refs[pos:pos + n_extra]
    pos += n_extra
    out_refs = refs[pos:pos + n_out]
    pos += n_out
    scratch = list(refs[pos:])
    r_ref = scratch.pop(0)
    a_ref = scratch.pop(0) if halo else None
    tm = xb_ref.shape[0]
    tn = w_refs[0].shape[1]
    hr = HALO_ROWS if halo else 0
    step = pl.program_id(0)
    tile = jnp.minimum(step, n_tiles - 1)

    def row_factor(ss):
        return lax.rsqrt(jnp.sum(ss, axis=-1, keepdims=True) * (1.0 / k_dim) + EPS)

    seq_start = ((tile // nj) * tm) % seq_len == 0 if halo else None

    @pl.when(tile % nj == 0)
    def _():
        r_ref[hr:hr + tm, :] = row_factor(ss_ref[...])
        if halo:
            a_ref[hr:hr + tm, :] = xb_ref[...]
            a_ref[0:hr, :] = xh_ref[...]
            r_ref[0:hr, :] = jnp.where(seq_start, 0.0, row_factor(sh_ref[...]))

    def products(cols):
        a = a_ref[...] if halo else xb_ref[...]
        r = r_ref[...]
        return [jnp.dot(a, w[:, cols], preferred_element_type=F32) * r for w in w_refs]

    if not lag:
        strip = min(tn, MXU_COLS)
        for c0 in range(0, tn, strip):
            cols = slice(c0, c0 + strip)
            epilogue(products(cols), extra_refs, out_refs, cols)
        return

    slots = [scratch[:n_streams], scratch[n_streams:]]
    everything = slice(0, tn)

    @pl.when(step == 0)
    def _():
        for acc_ref in slots[1]:
            acc_ref[...] = jnp.zeros(acc_ref.shape, F32)

    def phase(cur, prev):
        epilogue(slots[prev], extra_refs, out_refs, everything)
        for acc_ref, acc in zip(slots[cur], products(everything)):
            acc_ref[...] = acc

    @pl.when(step % 2 == 0)
    def _():
        phase(0, 1)

    @pl.when(step % 2 == 1)
    def _():
        phase(1, 0)


def projection(xb, ss, w, layer, *, name, col0, n_cols, n_streams, epilogue, outs, tn,
               extras=(), halo=False, lag=False, seq_len=None):
    t, k = xb.shape
    tm, tn = _tile(t, ROW_TILE), _tile(n_cols, tn)
    assert col0 % tn == 0
    nj, j0 = n_cols // tn, col0 // tn
    n_tiles = (t // tm) * nj
    hr = HALO_ROWS if halo else 0

    def cur(step):
        tile = jnp.minimum(step, n_tiles - 1)
        return tile // nj, tile % nj

    def out_tile(step):
        tile = jnp.maximum(step - 1, 0) if lag else step
        return tile // nj, tile % nj

    on_cur = lambda f: (lambda step: f(*cur(step)))
    on_out = lambda spec: pl.BlockSpec(spec.block_shape, lambda step, f=spec.index_map: f(*out_tile(step)))
    rows_mode = pl.Buffered(1) if (halo or tn >= 1024) else None
    in_specs = [pl.BlockSpec((tm, k), on_cur(lambda i, j: (i, 0)), pipeline_mode=rows_mode),
                pl.BlockSpec((tm, LANES), on_cur(lambda i, j: (i, 0)))]
    args = [xb, ss]
    scratch = [pltpu.VMEM((tm + hr, 1), F32)]
    if halo:
        assert seq_len % tm == 0 and tm % HALO_ROWS == 0
        per = tm // HALO_ROWS
        prev = on_cur(lambda i, j: (jnp.maximum(i * per - 1, 0), 0))
        in_specs += [pl.BlockSpec((HALO_ROWS, k), prev), pl.BlockSpec((HALO_ROWS, LANES), prev)]
        args += [xb, ss]
        scratch.append(pltpu.VMEM((tm + HALO_ROWS, k), BF16))
    if lag:
        scratch += [pltpu.VMEM((tm + hr, tn), F32)] * (2 * n_streams)
    for s in range(n_streams):
        in_specs.append(pl.BlockSpec((pl.Squeezed(), k, tn),
                                     on_cur(lambda i, j, s=s: (layer, 0, j0 + j + s * nj))))
        args.append(w)
    for arr, spec in extras:
        in_specs.append(on_out(spec))
        args.append(arr)
    body = functools.partial(_proj_kernel, n_streams=n_streams, n_extra=len(extras), n_out=len(outs),
                             epilogue=epilogue, halo=halo, lag=lag, seq_len=seq_len, k_dim=k, nj=nj,
                             n_tiles=n_tiles)
    return pl.pallas_call(
        body,
        grid=(n_tiles + int(lag),),
        in_specs=in_specs,
        out_specs=[on_out(spec) for _, spec in outs],
        out_shape=[shape for shape, _ in outs],
        scratch_shapes=scratch,
        compiler_params=_params("arbitrary"),
        name=name,
    )(*args)


def _row_col_tile(t, n, dtype, tn):
    tm, tn = _tile(t, ROW_TILE), _tile(n, tn)
    return jax.ShapeDtypeStruct((t, n), dtype), pl.BlockSpec((tm, tn), lambda i, j: (i, j))


def _col_vec(v, tn):
    return v, pl.BlockSpec((v.shape[0], tn), lambda i, j: (0, j))


def _conv_row_chunks(exts, w_ref, cols, row_chunk=CONV_ROW_CHUNK):
    width = w_ref.shape[0]
    tm = exts[0].shape[0] - HALO_ROWS
    chunk = _tile(tm, row_chunk)
    for r0 in range(0, tm, chunk):
        acc = None
        for tap in range(width):
            start = HALO_ROWS - (width - 1) + tap + r0
            term = functools.reduce(jnp.multiply, [e[start:start + chunk, :] for e in exts])
            term = term * w_ref[tap:tap + 1, cols]
            acc = term if acc is None else acc + term
        yield slice(r0, r0 + chunk), acc


def _heads(cols):
    return [(c // HEAD_DIM, slice(c, c + HEAD_DIM)) for c in range(cols.start, cols.stop, HEAD_DIM)]


def _swiglu_epilogue(accs, extra_refs, out_refs, cols):
    gate, up = accs
    out_refs[0][:, cols] = (_silu(gate) * up).astype(out_refs[0].dtype)


def _plain_epilogue(accs, extra_refs, out_refs, cols):
    out_refs[0][:, cols] = accs[0].astype(out_refs[0].dtype)


def _gdn_gates_kernel(a_ref, bt_ref, alog_ref, dtb_ref, gc_ref, beta_ref):
    c = GDN_CHUNK
    xs = a_ref[...] + dtb_ref[...]
    softplus = jnp.maximum(xs, 0.0) + jnp.log(1.0 + jnp.exp(-jnp.abs(xs)))
    g = -jnp.exp(alog_ref[...]) * softplus
    row = lax.broadcasted_iota(jnp.int32, (c, c), 0)
    col = lax.broadcasted_iota(jnp.int32, (c, c), 1)
    tri = (row >= col).astype(BF16)
    for r in range(0, g.shape[0], c):
        hi, mid, lo = _split3(g[r:r + c, :])
        gc_ref[r:r + c, :] = (jnp.dot(tri, hi, preferred_element_type=F32)
                              + jnp.dot(tri, mid, preferred_element_type=F32)
                              + jnp.dot(tri, lo, preferred_element_type=F32))
    beta_ref[...] = jax.nn.sigmoid(bt_ref[...])


def gdn_gates(ab, a_log, dt_bias, *, tb):
    t = ab.shape[0]
    h = a_log.shape[0]
    tb = _tile(t, tb)
    pad = lambda v: jnp.pad(v.astype(F32), (0, LANES - h)).reshape(1, LANES)
    row_blk = lambda j: pl.BlockSpec((tb, LANES), lambda i, j=j: (i, j))
    vec = pl.BlockSpec((1, LANES), lambda i: (0, 0))
    return pl.pallas_call(
        _gdn_gates_kernel,
        grid=(t // tb,),
        in_specs=[row_blk(0), row_blk(1), vec, vec],
        out_specs=[row_blk(0), row_blk(0)],
        out_shape=[jax.ShapeDtypeStruct((t, LANES), F32)] * 2,
        compiler_params=_params("parallel"),
        name="gdn_gates",
    )(ab, ab, pad(a_log), pad(dt_bias))


def _causal_conv(buf_ref, src, w_ref, seq_step):
    ts = src.shape[0]
    width = w_ref.shape[0]

    @pl.when(seq_step == 0)
    def _():
        buf_ref[0:CONV_HALO, :] = jnp.zeros((CONV_HALO, buf_ref.shape[1]), F32)

    buf_ref[CONV_HALO:CONV_HALO + ts, :] = src
    acc = None
    for tap in range(width):
        start = CONV_HALO - (width - 1) + tap
        term = buf_ref[start:start + ts, :] * w_ref[tap:tap + 1, :]
        acc = term if acc is None else acc + term
    buf_ref[0:CONV_HALO, :] = buf_ref[ts:ts + CONV_HALO, :]
    return acc


def _gdn_chunk_kernel(q_ref, k_ref, v_ref, z_ref, wq_ref, wk_ref, wv_ref, gcc_ref, bc_ref, gcr_ref,
                      onorm_ref, o_ref, bq_ref, bk_ref, bv_ref, state_ref, *, heads_per_step):
    c = GDN_CHUNK
    dk = HEAD_DIM
    seq_step = pl.program_id(2)
    ts = q_ref.shape[1]

    @pl.when(seq_step == 0)
    def _():
        state_ref[...] = jnp.zeros(state_ref.shape, F32)

    yq = _silu(_causal_conv(bq_ref, q_ref[0].astype(F32), wq_ref, seq_step))
    yk = _silu(_causal_conv(bk_ref, k_ref[0].astype(F32), wk_ref, seq_step))
    yv = _silu(_causal_conv(bv_ref, v_ref[0].astype(F32), wv_ref, seq_step))

    row = lax.broadcasted_iota(jnp.int32, (c, c), 0)
    col = lax.broadcasted_iota(jnp.int32, (c, c), 1)
    incl = row >= col
    strict = row > col
    eye = (row == col).astype(F32)
    lane = lax.broadcasted_iota(jnp.int32, (ts, LANES), 1)
    hb, nc = heads_per_step, ts // c

    per_head = []
    for i in range(hb):
        head = pl.program_id(1) * hb + i
        hl = slice(i * dk, (i + 1) * dk)
        qh, kh, vh = yq[:, hl], yk[:, hl], yv[:, hl]
        qh = qh * lax.rsqrt(jnp.sum(qh * qh, axis=-1, keepdims=True) + EPS) * (dk ** -0.5)
        kh = kh * lax.rsqrt(jnp.sum(kh * kh, axis=-1, keepdims=True) + EPS)
        pick = lane == head
        gcol = jnp.sum(jnp.where(pick, gcc_ref[0], 0.0), axis=-1, keepdims=True)
        bcol = jnp.sum(jnp.where(pick, bc_ref[0], 0.0), axis=-1, keepdims=True)
        grow = gcr_ref[0, pl.ds(head, 1), :]
        per_head.append((qh, kh, vh, gcol, bcol, grow))
    items = [(ci, i) for ci in range(nc) for i in range(hb)]
    qs, ks, vs, gcols, bcols, grows = [], [], [], [], [], []
    for ci, i in items:
        qh, kh, vh, gcol, bcol, grow = per_head[i]
        sl = slice(ci * c, (ci + 1) * c)
        qs.append(qh[sl])
        ks.append(kh[sl])
        vs.append(vh[sl])
        gcols.append(gcol[sl])
        bcols.append(bcol[sl])
        grows.append(grow[:, sl])
    n = len(items)
    rng = range(n)
    gammas = [jnp.where(incl, jnp.exp(jnp.where(incl, gcols[t] - grows[t], 0.0)), 0.0) for t in rng]
    kbs = [ks[t] * bcols[t] for t in rng]
    n_pows = [jnp.where(strict, -(_dot_nt(kbs[t], ks[t]) * gammas[t]), 0.0) for t in rng]
    t_invs = [eye + n_pows[t] for t in rng]
    for _ in range(5):
        n_pows = [_dot(n_pows[t], n_pows[t]) for t in rng]
        t_invs = [t_invs[t] + _dot(t_invs[t], n_pows[t]) for t in rng]
    e_gs = [jnp.exp(gcols[t]) for t in rng]
    wus = [_dot(t_invs[t], jnp.concatenate([kbs[t] * e_gs[t], vs[t] * bcols[t]], axis=1)) for t in rng]
    qks = [jnp.where(incl, _dot_nt(qs[t], ks[t]) * gammas[t], 0.0) for t in rng]
    g_lasts = [grows[t][:, c - 1:c] for t in rng]
    q_decs = [qs[t] * e_gs[t] for t in rng]
    k_decs = [ks[t] * jnp.exp(g_lasts[t] - gcols[t]) for t in rng]
    decays = [jnp.exp(g_lasts[t]) for t in rng]

    states = [state_ref[i] for i in range(hb)]
    outs = [None] * n
    for ci in range(nc):
        idx = [ci * hb + i for i in range(hb)]
        v_news = [wus[t][:, dk:] - _dot(wus[t][:, :dk], states[i]) for i, t in enumerate(idx)]
        o_state = [_dot(q_decs[t], states[i]) for i, t in enumerate(idx)]
        for i, t in enumerate(idx):
            outs[t] = o_state[i] + _dot(qks[t], v_news[i])
        states = [states[i] * decays[t] + _dot_tn(k_decs[t], v_news[i]) for i, t in enumerate(idx)]
    for i in range(hb):
        state_ref[i] = states[i]

    for t, (ci, i) in enumerate(items):
        o = outs[t]
        o = o * lax.rsqrt(jnp.mean(o * o, axis=-1, keepdims=True) + EPS) * onorm_ref[...]
        z = z_ref[0, ci * c:(ci + 1) * c, i * dk:(i + 1) * dk].astype(F32)
        o_ref[0, ci * c:(ci + 1) * c, i * dk:(i + 1) * dk] = (o * _silu(z)).astype(o_ref.dtype)


def gdn_chunk(proj, conv_w, gcc, betac, gcr, o_norm, *, n_heads, ts, heads_per_step):
    b, s, _ = proj.shape
    dk = HEAD_DIM
    ts = _tile(s, ts)
    hb = _tile(n_heads, heads_per_step)
    groups = n_heads // hb
    assert ts % GDN_CHUNK == 0 and conv_w.shape[0] - 1 <= CONV_HALO
    width = conv_w.shape[0]
    tok = lambda part: pl.BlockSpec((1, ts, hb * dk), lambda bi, hi, si, p=part: (bi, si, hi + p * groups))
    cw = lambda part: pl.BlockSpec((width, hb * dk), lambda bi, hi, si, p=part: (0, hi + p * groups))
    gate_col = pl.BlockSpec((1, ts, LANES), lambda bi, hi, si: (bi, si, 0))
    return pl.pallas_call(
        functools.partial(_gdn_chunk_kernel, heads_per_step=hb),
        grid=(b, groups, s // ts),
        in_specs=[tok(0), tok(1), tok(2), tok(3), cw(0), cw(1), cw(2), gate_col, gate_col,
                  pl.BlockSpec((1, n_heads, ts), lambda bi, hi, si: (bi, 0, si)),
                  pl.BlockSpec((1, dk), lambda bi, hi, si: (0, 0))],
        out_specs=pl.BlockSpec((1, ts, hb * dk), lambda bi, hi, si: (bi, si, hi)),
        out_shape=jax.ShapeDtypeStruct((b, s, n_heads * dk), BF16),
        scratch_shapes=[pltpu.VMEM((CONV_HALO + ts, hb * dk), F32)] * 3
                       + [pltpu.VMEM((hb, dk, dk), F32)],
        compiler_params=_params("parallel", "parallel", "arbitrary"),
        name="gdn_chunk",
    )(proj, proj, proj, proj, conv_w, conv_w, conv_w, gcc, betac, gcr,
      o_norm.reshape(1, dk).astype(F32))


def gdn_split_weights(w_in, n_heads):
    wide = 4 * n_heads * HEAD_DIM
    pad = ((0, 0), (0, 0), (0, LANES - n_heads))
    w_a = jnp.pad(w_in[:, :, wide:wide + n_heads], pad)
    w_b = jnp.pad(w_in[:, :, wide + n_heads:], pad)
    return w_in[:, :, :wide].astype(BF16), jnp.concatenate([w_a, w_b], axis=2).astype(BF16)


def gated_deltanet(x, xb, ss, w_big, w_small, layer, conv_w, a_log, dt_bias, o_norm, w_out,
                   next_gain, batch):
    t, d = x.shape
    n_heads = a_log.shape[0]
    s = t // batch
    wide = 4 * n_heads * HEAD_DIM
    (proj,) = projection(xb, ss, w_big, layer, name="gdn_in_proj", col0=0, n_cols=wide, n_streams=1,
                         epilogue=_plain_epilogue, outs=[_row_col_tile(t, wide, BF16, 1024)], tn=1024)
    (ab,) = projection(xb, ss, w_small, layer, name="gdn_ab_proj", col0=0, n_cols=2 * LANES,
                       n_streams=1, epilogue=_plain_epilogue,
                       outs=[_row_col_tile(t, 2 * LANES, F32, 2 * LANES)], tn=2 * LANES)
    gcc, betac = gdn_gates(ab, a_log, dt_bias, tb=512)
    gcc = gcc.reshape(batch, s, LANES)
    betac = betac.reshape(batch, s, LANES)
    gcr = jnp.swapaxes(gcc[:, :, :n_heads], 1, 2)
    o = gdn_chunk(proj.reshape(batch, s, wide), conv_w.astype(F32), gcc, betac, gcr, o_norm,
                  n_heads=n_heads, ts=256, heads_per_step=8)
    return matmul_residual(o.reshape(t, n_heads * HEAD_DIM), w_out, layer, x, next_gain,
                           name="gdn_out_proj", scale=1.0)


def _shortconv_epilogue(accs, extra_refs, out_refs, cols):
    gb, gc, xv = accs
    for rows, conv in _conv_row_chunks([gc, xv], extra_refs[0], cols):
        gate = gb[HALO_ROWS + rows.start:HALO_ROWS + rows.stop, :]
        out_refs[0][rows, cols] = (gate * conv).astype(out_refs[0].dtype)


def short_conv(x, xb, ss, w_in, layer, conv_w, w_out, next_gain, batch):
    t, d = x.shape
    tn = 256
    (y,) = projection(xb, ss, w_in, layer, name="sc_in_proj", col0=0, n_cols=d, n_streams=3,
                      epilogue=_shortconv_epilogue, outs=[_row_col_tile(t, d, BF16, tn)], tn=tn,
                      extras=[_col_vec(conv_w.astype(F32), tn)], halo=True, seq_len=t // batch)
    return matmul_residual(y, w_out, layer, x, next_gain, name="sc_out_proj", scale=1.0)


def _rope_table_kernel(pos_ref, freq_ref, cos_ref, sin_ref):
    ang = pos_ref[0].astype(F32) * freq_ref[...]
    lane = lax.broadcasted_iota(jnp.int32, ang.shape, 1)
    sign = jnp.where(lane < HEAD_DIM // 2, -1.0, 1.0)
    cos_ref[0] = jnp.cos(ang)
    sin_ref[0] = jnp.sin(ang) * sign


def rope_tables(positions, *, ts):
    b, s = positions.shape
    ts = _tile(s, ts)
    half = HEAD_DIM // 2
    inv_freq = ROPE_THETA ** (-jnp.arange(half, dtype=F32) / half)
    freq = jnp.concatenate([inv_freq, inv_freq]).reshape(1, HEAD_DIM)
    blk = pl.BlockSpec((1, ts, HEAD_DIM), lambda bi, si: (bi, si, 0))
    return pl.pallas_call(
        _rope_table_kernel,
        grid=(b, s // ts),
        in_specs=[pl.BlockSpec((1, ts, 1), lambda bi, si: (bi, si, 0)),
                  pl.BlockSpec((1, HEAD_DIM), lambda bi, si: (0, 0))],
        out_specs=[blk, blk],
        out_shape=[jax.ShapeDtypeStruct((b, s, HEAD_DIM), F32)] * 2,
        compiler_params=_params("parallel", "parallel"),
        name="rope_tables",
    )(positions.reshape(b, s, 1), freq)


def _norm_rope(seg, cos, sin, gain):
    seg = seg * lax.rsqrt(jnp.mean(seg * seg, axis=-1, keepdims=True) + EPS) * gain
    return seg * cos + pltpu.roll(seg, HEAD_DIM // 2, axis=1) * sin


def _moba_q_epilogue(accs, extra_refs, out_refs, cols):
    cos_ref, sin_ref, gain_ref = extra_refs
    cos, sin, gain = cos_ref[...], sin_ref[...], gain_ref[...]
    acc = accs[0]
    for _, hl in _heads(cols):
        seg = acc[:, hl.start - cols.start:hl.stop - cols.start]
        qh = _norm_rope(seg, cos, sin, gain) * (HEAD_DIM ** -0.5 * LOG2_E)
        out_refs[0][:, hl] = qh.astype(out_refs[0].dtype)


def _moba_k_epilogue(accs, extra_refs, out_refs, cols):
    cos_ref, sin_ref, gain_ref = extra_refs
    cos, sin, gain = cos_ref[...], sin_ref[...], gain_ref[...]
    ko_ref, km_ref = out_refs
    acc = accs[0]
    bs = MOBA_BLOCK
    for h, hl in _heads(cols):
        kh = _norm_rope(acc[:, hl.start - cols.start:hl.stop - cols.start], cos, sin, gain)
        for g in range(acc.shape[0] // bs):
            blk = kh[g * bs:(g + 1) * bs]
            ko_ref[0, h, g] = blk.astype(ko_ref.dtype)
            km_ref[g, :, hl] = jnp.mean(blk, axis=0, keepdims=True)


def _moba_v_epilogue(accs, extra_refs, out_refs, cols):
    vto_ref = out_refs[0]
    acc = accs[0]
    bs, dh = MOBA_BLOCK, HEAD_DIM
    sub = lax.broadcasted_iota(jnp.int32, (SUM_ROWS, bs), 0)
    ones_rows = jnp.where(sub == 0, 1.0, 0.0).astype(vto_ref.dtype)
    for h, hl in _heads(cols):
        for g in range(acc.shape[0] // bs):
            blk = acc[g * bs:(g + 1) * bs, hl.start - cols.start:hl.stop - cols.start]
            vto_ref[0, h, g, 0:dh, :] = blk.T.astype(vto_ref.dtype)
            vto_ref[0, h, g, dh:dh + SUM_ROWS, :] = ones_rows


def _moba_attn_kernel(q_ref, k_ref, vt_ref, km_ref, o_ref, bias_ref, *, group, slab, heads):
    bs, dh = MOBA_BLOCK, HEAD_DIM
    nb = km_ref.shape[2]
    qi = pl.program_id(2)
    hrange = range(heads)
    q = [q_ref[0, :, h * dh:(h + 1) * dh] for h in hrange]

    blk = lax.broadcasted_iota(jnp.int32, (nb, bs), 0)
    blk_f = blk.astype(F32)
    past = blk < qi
    for h in hrange:
        gate = functools.reduce(jnp.add, [_dot_nt(part, q[h]) for part in _split3(km_ref[0, h])])
        gate = jnp.where(past, gate, -jnp.inf)
        bias = jnp.full((nb, bs), -jnp.inf, F32)
        for _ in range(MOBA_TOPK):
            top = jnp.max(gate, axis=0, keepdims=True)
            first = jnp.min(jnp.where(gate == top, blk_f, float(nb)), axis=0, keepdims=True)
            hit = (blk_f == first) & (top > -jnp.inf)
            bias = jnp.where(hit, 0.0, bias)
            gate = jnp.where(hit, -jnp.inf, gate)
        bias_ref[h] = bias

    def slab_scores(h, j, n):
        return _dot_nt(k_ref[0, h, pl.ds(j, n)].reshape(n * bs, dh), q[h])

    def pieces(s_t):
        m_blk = jnp.max(s_t, axis=0, keepdims=True)
        return m_blk, jnp.exp2(s_t - m_blk).astype(BF16)

    def values(h, j, p):
        return jnp.dot(vt_ref[0, h, j], p, preferred_element_type=F32)[:dh + 8]

    def merge(m, acc, parts):
        m_new = functools.reduce(jnp.maximum, [m] + [mb for mb, _ in parts])
        acc = jnp.exp2(m - m_new) * acc
        for mb, ab in parts:
            acc = acc + jnp.exp2(mb - m_new) * ab
        return m_new, acc

    key_pos = lax.broadcasted_iota(jnp.int32, (bs, bs), 0)
    qry_pos = lax.broadcasted_iota(jnp.int32, (bs, bs), 1)
    own = [slab_scores(h, qi, 1) for h in hrange]
    own = [pieces(jnp.where(key_pos <= qry_pos, s_t, -jnp.inf)) for s_t in own]
    carry = tuple((own[h][0], values(h, qi, own[h][1])) for h in hrange)

    def body(it, carry):
        j0 = it * group
        slabs = [[slab_scores(h, j0 + u * slab, slab) for u in range(group // slab)] for h in hrange]
        mp = [[pieces(slabs[h][b // slab][(b % slab) * bs:(b % slab + 1) * bs]) for b in range(group)]
              for h in hrange]
        parts = [[(mp[h][b][0] + bias_ref[h, pl.ds(j0 + b, 1), :], values(h, j0 + b, mp[h][b][1]))
                  for b in range(group)] for h in hrange]
        return tuple(merge(*carry[h], parts[h]) for h in hrange)

    carry = lax.fori_loop(0, (qi + group - 1) // group, body, carry)
    for h in hrange:
        acc = carry[h][1]
        o_ref[0, :, h * dh:(h + 1) * dh] = (acc[:dh] / acc[dh:dh + 1]).T.astype(o_ref.dtype)


def moba_attn(q, k, vt, kmean):
    b, s, d = q.shape
    _, n_heads, nb, bs, dh = k.shape
    group = _tile(nb, 8)
    slab = _tile(group, 2)
    heads = _tile(n_heads, 4)
    return pl.pallas_call(
        functools.partial(_moba_attn_kernel, group=group, slab=slab, heads=heads),
        grid=(b, n_heads // heads, nb),
        in_specs=[pl.BlockSpec((1, bs, heads * dh), lambda bi, hi, qi: (bi, qi, hi)),
                  pl.BlockSpec((1, heads, nb, bs, dh), lambda bi, hi, qi: (bi, hi, 0, 0, 0)),
                  pl.BlockSpec((1, heads, nb, dh + SUM_ROWS, bs), lambda bi, hi, qi: (bi, hi, 0, 0, 0)),
                  pl.BlockSpec((1, heads, nb, dh), lambda bi, hi, qi: (bi, hi, 0, 0))],
        out_specs=pl.BlockSpec((1, bs, heads * dh), lambda bi, hi, qi: (bi, qi, hi)),
        out_shape=jax.ShapeDtypeStruct((b, s, d), BF16),
        scratch_shapes=[pltpu.VMEM((heads, nb, bs), F32)],
        compiler_params=_params("parallel", "parallel", "arbitrary"),
        name="moba_attn",
    )(q, k, vt, kmean)


def moba_attention(x, xb, ss, positions, w_in, layer, q_norm, k_norm, w_out, next_gain):
    t, d = x.shape
    batch, s = positions.shape
    n_heads = d // HEAD_DIM
    dh, bs = HEAD_DIM, MOBA_BLOCK
    assert s % bs == 0
    nb = s // bs
    tm = _tile(t, ROW_TILE)
    assert tm % bs == 0 and s % tm == 0
    tn = _tile(d, 512)
    hpt, bpt, tiles_per_seq = tn // dh, tm // bs, s // tm
    cos, sin = rope_tables(positions, ts=512)
    rows = pl.BlockSpec((tm, dh), lambda i, j: (i, 0))
    vec = pl.BlockSpec((1, dh), lambda i, j: (0, 0))
    rope_extras = lambda gain: [(cos.reshape(t, dh), rows), (sin.reshape(t, dh), rows),
                                (gain.reshape(1, dh).astype(F32), vec)]
    blocked = lambda i, j: (i // tiles_per_seq, j, i % tiles_per_seq, 0, 0)
    (q,) = projection(xb, ss, w_in, layer, name="moba_q_proj", col0=0, n_cols=d, n_streams=1,
                      epilogue=_moba_q_epilogue, outs=[_row_col_tile(t, d, BF16, tn)], tn=tn,
                      extras=rope_extras(q_norm), lag=True)
    k, kmean = projection(
        xb, ss, w_in, layer, name="moba_k_proj", col0=d, n_cols=d, n_streams=1,
        epilogue=_moba_k_epilogue, tn=tn, extras=rope_extras(k_norm), lag=True,
        outs=[(jax.ShapeDtypeStruct((batch, n_heads, nb, bs, dh), BF16),
               pl.BlockSpec((1, hpt, bpt, bs, dh), blocked)),
              (jax.ShapeDtypeStruct((t // bs, 1, d), F32),
               pl.BlockSpec((bpt, 1, tn), lambda i, j: (i, 0, j)))])
    (vt,) = projection(
        xb, ss, w_in, layer, name="moba_v_proj", col0=2 * d, n_cols=d, n_streams=1,
        epilogue=_moba_v_epilogue, tn=tn,
        outs=[(jax.ShapeDtypeStruct((batch, n_heads, nb, dh + SUM_ROWS, bs), BF16),
               pl.BlockSpec((1, hpt, bpt, dh + SUM_ROWS, bs), blocked))])
    kmean = kmean.reshape(batch, nb, n_heads, dh).transpose(0, 2, 1, 3)
    o = moba_attn(q.reshape(batch, s, d), k, vt, kmean)
    return matmul_residual(o.reshape(t, d), w_out, layer, x, next_gain, name="moba_out_proj", scale=1.0)


def swiglu_half_step(x, xb, ss, w_in, w_out, layer, next_gain):
    t = x.shape[0]
    f = w_in.shape[2] // 2
    tn = 512
    (act,) = projection(xb, ss, w_in, layer, name="ffn_in", col0=0, n_cols=f, n_streams=2,
                        epilogue=_swiglu_epilogue, outs=[_row_col_tile(t, f, BF16, tn)], tn=tn)
    return matmul_residual(act, w_out, layer, x, next_gain, name="ffn_out", scale=0.5)


def kernel(x, positions, norm_ffn1, norm_mix, norm_ffn2, ffn1_w_in, ffn1_w_out, ffn2_w_in, ffn2_w_out, gdn_w_in, gdn_conv_w, gdn_a_log, gdn_dt_bias, gdn_out_norm, gdn_w_out, sc_w_in, sc_conv_w, sc_w_out, moba_w_in, moba_q_norm, moba_k_norm, moba_w_out):
    batch, s, d = x.shape
    depth = norm_ffn1.shape[0]
    ffn1_w_in, ffn1_w_out, ffn2_w_in, ffn2_w_out, gdn_w_out, sc_w_in, sc_w_out, moba_w_in, moba_w_out = (
        w.astype(BF16) for w in (ffn1_w_in, ffn1_w_out, ffn2_w_in, ffn2_w_out, gdn_w_out, sc_w_in,
                                 sc_w_out, moba_w_in, moba_w_out))
    gdn_w_big, gdn_w_small = gdn_split_weights(gdn_w_in, gdn_a_log.shape[1])
    xt = x.reshape(batch * s, d)
    xb, ss = scale_rows(xt, norm_ffn1[0], tm=256)
    for i in range(depth):
        kind, j = i % 3, i // 3
        after_layer = norm_ffn1[i + 1] if i + 1 < depth else None
        xt, xb, ss = swiglu_half_step(xt, xb, ss, ffn1_w_in, ffn1_w_out, i, norm_mix[i])
        if kind == 0:
            xt, xb, ss = gated_deltanet(xt, xb, ss, gdn_w_big, gdn_w_small, j, gdn_conv_w[j], gdn_a_log[j],
                                        gdn_dt_bias[j], gdn_out_norm[j], gdn_w_out, norm_ffn2[i], batch)
        elif kind == 1:
            xt, xb, ss = short_conv(xt, xb, ss, sc_w_in, j, sc_conv_w[j], sc_w_out, norm_ffn2[i], batch)
        else:
            xt, xb, ss = moba_attention(xt, xb, ss, positions, moba_w_in, j, moba_q_norm[j],
                                        moba_k_norm[j], moba_w_out, norm_ffn2[i])
        xt, xb, ss = swiglu_half_step(xt, xb, ss, ffn2_w_in, ffn2_w_out, i, after_layer)
    return xt.reshape(batch, s, d)
```

```python
import functools

import jax
import jax.numpy as jnp
from jax import lax
from jax.experimental import pallas as pl
from jax.experimental.pallas import tpu as pltpu

F32 = jnp.float32
BF16 = jnp.bfloat16

EPS = 1e-6
HEAD_DIM = 128
GDN_CHUNK = 64
GDN_CONV_TAPS = 4
MOBA_BLOCK = 256
MOBA_TOPK = 3
ROPE_THETA = 10000.0
LOG2_E = 1.4426950408889634
SUM_ROWS = 16

LANES = 128
MXU_COLS = 256
HALO_ROWS = 16
VMEM_LIMIT = 56 * 1024 * 1024
ROW_TILE = 1024
RES_ROW_TILE = 1024
RES_COL_TILE = 512
CONV_ROW_CHUNK = 128


def _tile(dim, want):
    t = min(dim, want)
    while dim % t:
        t //= 2
    return t


def _params(*sem):
    return pltpu.CompilerParams(dimension_semantics=sem, vmem_limit_bytes=VMEM_LIMIT)


def _dot(a, b):
    return jnp.dot(a.astype(BF16), b.astype(BF16), preferred_element_type=F32)


def _dot_nt(a, b):
    return lax.dot_general(a.astype(BF16), b.astype(BF16), (((1,), (1,)), ((), ())),
                           preferred_element_type=F32)


def _dot_tn(a, b):
    return lax.dot_general(a.astype(BF16), b.astype(BF16), (((0,), (0,)), ((), ())),
                           preferred_element_type=F32)


def _split3(x):
    hi = x.astype(BF16)
    r1 = x - hi.astype(F32)
    mid = r1.astype(BF16)
    lo = (r1 - mid.astype(F32)).astype(BF16)
    return hi, mid, lo


def _silu(x):
    return x * jax.nn.sigmoid(x)


def _fold_lanes(x):
    parts = [x[:, c * LANES:(c + 1) * LANES] for c in range(x.shape[1] // LANES)]
    return functools.reduce(jnp.add, parts)


def _scale_rows_kernel(x_ref, g_ref, xb_ref, ss_ref):
    x = x_ref[...]
    xb_ref[...] = (x * g_ref[...]).astype(BF16)
    ss_ref[...] = _fold_lanes(x * x)


def scale_rows(x, gain, *, tm):
    t, d = x.shape
    tm = _tile(t, tm)
    return pl.pallas_call(
        _scale_rows_kernel,
        grid=(t // tm,),
        in_specs=[pl.BlockSpec((tm, d), lambda i: (i, 0)), pl.BlockSpec((1, d), lambda i: (0, 0))],
        out_specs=[pl.BlockSpec((tm, d), lambda i: (i, 0)), pl.BlockSpec((tm, LANES), lambda i: (i, 0))],
        out_shape=[jax.ShapeDtypeStruct((t, d), BF16), jax.ShapeDtypeStruct((t, LANES), F32)],
        compiler_params=_params("parallel"),
        name="scale_rows",
    )(x, gain.reshape(1, d).astype(F32))


def _mm_residual_kernel(*refs, scale, emit_next):
    if emit_next:
        a_ref, w_ref, x_ref, g_ref, o_ref, xb_ref, ss_ref = refs
    else:
        a_ref, w_ref, x_ref, o_ref = refs
    xn = x_ref[...] + scale * jnp.dot(a_ref[...], w_ref[...], preferred_element_type=F32)
    o_ref[...] = xn
    if emit_next:
        xb_ref[...] = (xn * g_ref[...]).astype(BF16)
        part = _fold_lanes(xn * xn)

        @pl.when(pl.program_id(1) == 0)
        def _():
            ss_ref[...] = part

        @pl.when(pl.program_id(1) != 0)
        def _():
            ss_ref[...] += part


def matmul_residual(a, w, layer, x, next_gain, *, name, scale):
    t, k = a.shape
    n = w.shape[2]
    tm, tn = _tile(t, RES_ROW_TILE), _tile(n, RES_COL_TILE)
    emit_next = next_gain is not None
    tile = pl.BlockSpec((tm, tn), lambda i, j: (i, j))
    in_specs = [pl.BlockSpec((tm, k), lambda i, j: (i, 0)),
                pl.BlockSpec((pl.Squeezed(), k, tn), lambda i, j: (layer, 0, j)),
                tile]
    args = [a, w, x]
    out_specs, out_shape = [tile], [jax.ShapeDtypeStruct((t, n), F32)]
    if emit_next:
        in_specs.append(pl.BlockSpec((1, tn), lambda i, j: (0, j)))
        args.append(next_gain.reshape(1, n).astype(F32))
        out_specs += [tile, pl.BlockSpec((tm, LANES), lambda i, j: (i, 0))]
        out_shape += [jax.ShapeDtypeStruct((t, n), BF16), jax.ShapeDtypeStruct((t, LANES), F32)]
    outs = pl.pallas_call(
        functools.partial(_mm_residual_kernel, scale=scale, emit_next=emit_next),
        grid=(t // tm, n // tn),
        in_specs=in_specs,
        out_specs=out_specs,
        out_shape=out_shape,
        compiler_params=_params("parallel", "arbitrary"),
        name=name,
    )(*args)
    return tuple(outs) if emit_next else (outs[0], None, None)


def _proj_kernel(*refs, n_streams, n_extra, n_out, epilogue, halo, lag, seq_len, k_dim, nj, n_tiles):
    xb_ref, ss_ref = refs[0], refs[1]
    pos = 2
    if halo:
        xh_ref, sh_ref = refs[2], refs[3]
        pos = 4
    w_refs = refs[pos:pos + n_streams]
    pos += n_streams
    extra_refs = refs[pos:pos + n_extra]
    pos += n_extra
    out_refs = refs[pos:pos + n_out]
    pos += n_out
    scratch = list(refs[pos:])
    r_ref = scratch.pop(0)
    a_ref = scratch.pop(0) if halo else None
    tm = xb_ref.shape[0]
    tn = w_refs[0].shape[1]
    hr = HALO_ROWS if halo else 0
    step = pl.program_id(0)
    tile = jnp.minimum(step, n_tiles - 1)

    def row_factor(ss):
        return lax.rsqrt(jnp.sum(ss, axis=-1, keepdims=True) * (1.0 / k_dim) + EPS)

    seq_start = ((tile // nj) * tm) % seq_len == 0 if halo else None

    @pl.when(tile % nj == 0)
    def _():
        r_ref[hr:hr + tm, :] = row_factor(ss_ref[...])
        if halo:
            a_ref[hr:hr + tm, :] = xb_ref[...]
            a_ref[0:hr, :] = xh_ref[...]
            r_ref[0:hr, :] = jnp.where(seq_start, 0.0, row_factor(sh_ref[...]))

    def products(cols):
        a = a_ref[...] if halo else xb_ref[...]
        r = r_ref[...]
        return [jnp.dot(a, w[:, cols], preferred_element_type=F32) * r for w in w_refs]

    if not lag:
        strip = min(tn, MXU_COLS)
        for c0 in range(0, tn, strip):
            cols = slice(c0, c0 + strip)
            epilogue(products(cols), extra_refs, out_refs, cols)
        return

    slots = [scratch[:n_streams], scratch[n_streams:]]
    everything = slice(0, tn)

    @pl.when(step == 0)
    def _():
        for acc_ref in slots[1]:
            acc_ref[...] = jnp.zeros(acc_ref.shape, F32)

    def phase(cur, prev):
        epilogue(slots[prev], extra_refs, out_refs, everything)
        for acc_ref, acc in zip(slots[cur], products(everything)):
            acc_ref[...] = acc

    @pl.when(step % 2 == 0)
    def _():
        phase(0, 1)

    @pl.when(step % 2 == 1)
    def _():
        phase(1, 0)


def projection(xb, ss, w, layer, *, name, col0, n_cols, n_streams, epilogue, outs, tn,
               extras=(), halo=False, lag=False, seq_len=None):
    t, k = xb.shape
    tm, tn = _tile(t, ROW_TILE), _tile(n_cols, tn)
    assert col0 % tn == 0
    nj, j0 = n_cols // tn, col0 // tn
    n_tiles = (t // tm) * nj
    hr = HALO_ROWS if halo else 0

    def cur(step):
        tile = jnp.minimum(step, n_tiles - 1)
        return tile // nj, tile % nj

    def out_tile(step):
        tile = jnp.maximum(step - 1, 0) if lag else step
        return tile // nj, tile % nj

    on_cur = lambda f: (lambda step: f(*cur(step)))
    on_out = lambda spec: pl.BlockSpec(spec.block_shape, lambda step, f=spec.index_map: f(*out_tile(step)))
    rows_mode = pl.Buffered(1) if (halo or tn >= 1024) else None
    in_specs = [pl.BlockSpec((tm, k), on_cur(lambda i, j: (i, 0)), pipeline_mode=rows_mode),
                pl.BlockSpec((tm, LANES), on_cur(lambda i, j: (i, 0)))]
    args = [xb, ss]
    scratch = [pltpu.VMEM((tm + hr, 1), F32)]
    if halo:
        assert seq_len % tm == 0 and tm % HALO_ROWS == 0
        per = tm // HALO_ROWS
        prev = on_cur(lambda i, j: (jnp.maximum(i * per - 1, 0), 0))
        in_specs += [pl.BlockSpec((HALO_ROWS, k), prev), pl.BlockSpec((HALO_ROWS, LANES), prev)]
        args += [xb, ss]
        scratch.append(pltpu.VMEM((tm + HALO_ROWS, k), BF16))
    if lag:
        scratch += [pltpu.VMEM((tm + hr, tn), F32)] * (2 * n_streams)
    for s in range(n_streams):
        in_specs.append(pl.BlockSpec((pl.Squeezed(), k, tn),
                                     on_cur(lambda i, j, s=s: (layer, 0, j0 + j + s * nj))))
        args.append(w)
    for arr, spec in extras:
        in_specs.append(on_out(spec))
        args.append(arr)
    body = functools.partial(_proj_kernel, n_streams=n_streams, n_extra=len(extras), n_out=len(outs),
                             epilogue=epilogue, halo=halo, lag=lag, seq_len=seq_len, k_dim=k, nj=nj,
                             n_tiles=n_tiles)
    return pl.pallas_call(
        body,
        grid=(n_tiles + int(lag),),
        in_specs=in_specs,
        out_specs=[on_out(spec) for _, spec in outs],
        out_shape=[shape for shape, _ in outs],
        scratch_shapes=scratch,
        compiler_params=_params("arbitrary"),
        name=name,
    )(*args)


def _row_col_tile(t, n, dtype, tn):
    tm, tn = _tile(t, ROW_TILE), _tile(n, tn)
    return jax.ShapeDtypeStruct((t, n), dtype), pl.BlockSpec((tm, tn), lambda i, j: (i, j))


def _col_vec(v, tn):
    return v, pl.BlockSpec((v.shape[0], tn), lambda i, j: (0, j))


def _conv_row_chunks(exts, w_ref, cols, row_chunk=CONV_ROW_CHUNK):
    width = w_ref.shape[0]
    tm = exts[0].shape[0] - HALO_ROWS
    chunk = _tile(tm, row_chunk)
    for r0 in range(0, tm, chunk):
        acc = None
        for tap in range(width):
            start = HALO_ROWS - (width - 1) + tap + r0
            term = functools.reduce(jnp.multiply, [e[start:start + chunk, :] for e in exts])
            term = term * w_ref[tap:tap + 1, cols]
            acc = term if acc is None else acc + term
        yield slice(r0, r0 + chunk), acc


def _heads(cols):
    return [(c // HEAD_DIM, slice(c, c + HEAD_DIM)) for c in range(cols.start, cols.stop, HEAD_DIM)]


def _swiglu_epilogue(accs, extra_refs, out_refs, cols):
    gate, up = accs
    out_refs[0][:, cols] = (_silu(gate) * up).astype(out_refs[0].dtype)


def _plain_epilogue(accs, extra_refs, out_refs, cols):
    out_refs[0][:, cols] = accs[0].astype(out_refs[0].dtype)


def _gdn_gates_kernel(ab_ref, alog_ref, dtb_ref, gc_ref, beta_ref, *, n_heads):
    c = GDN_CHUNK
    xs = ab_ref[...] + dtb_ref[...]
    softplus = jnp.maximum(xs, 0.0) + jnp.log(1.0 + jnp.exp(-jnp.abs(xs)))
    g = -jnp.exp(alog_ref[...]) * softplus
    row = lax.broadcasted_iota(jnp.int32, (c, c), 0)
    col = lax.broadcasted_iota(jnp.int32, (c, c), 1)
    tri = (row >= col).astype(BF16)
    for r in range(0, g.shape[0], c):
        hi, mid, lo = _split3(g[r:r + c, :])
        gc_ref[r:r + c, :] = (jnp.dot(tri, hi, preferred_element_type=F32)
                              + jnp.dot(tri, mid, preferred_element_type=F32)
                              + jnp.dot(tri, lo, preferred_element_type=F32))
    beta_ref[...] = jax.nn.sigmoid(pltpu.roll(ab_ref[...], LANES - n_heads, axis=1))


def gdn_gates(ab, a_log, dt_bias, *, tb):
    t = ab.shape[0]
    h = a_log.shape[0]
    assert 2 * h <= LANES
    tb = _tile(t, tb)
    pad = lambda v: jnp.pad(v.astype(F32), (0, LANES - h)).reshape(1, LANES)
    row_blk = pl.BlockSpec((tb, LANES), lambda i: (i, 0))
    vec = pl.BlockSpec((1, LANES), lambda i: (0, 0))
    return pl.pallas_call(
        functools.partial(_gdn_gates_kernel, n_heads=h),
        grid=(t // tb,),
        in_specs=[row_blk, vec, vec],
        out_specs=[row_blk, row_blk],
        out_shape=[jax.ShapeDtypeStruct((t, LANES), F32)] * 2,
        compiler_params=_params("parallel"),
        name="gdn_gates",
    )(ab, pad(a_log), pad(dt_bias))


def _shift_matrices(ts):
    n = GDN_CONV_TAPS - 1
    r = lax.broadcasted_iota(jnp.int32, (n * ts, ts), 0)
    c = lax.broadcasted_iota(jnp.int32, (n * ts, ts), 1)
    inner = (c + r // ts + 1 == r % ts).astype(BF16)
    r = lax.broadcasted_iota(jnp.int32, (n * HALO_ROWS, HALO_ROWS), 0)
    c = lax.broadcasted_iota(jnp.int32, (n * HALO_ROWS, HALO_ROWS), 1)
    edge = (c == HALO_ROWS - (r // HALO_ROWS + 1) + r % HALO_ROWS).astype(BF16)
    return inner, edge


def _causal_conv(x_ref, halo_ref, inner_ref, edge_ref, w_ref):
    x = x_ref[0]
    ts = x.shape[0]
    width = w_ref.shape[0]
    shifted = jnp.dot(inner_ref[...], x, preferred_element_type=F32)
    carried = jnp.dot(edge_ref[...], halo_ref[...], preferred_element_type=F32)
    acc = x.astype(F32) * w_ref[width - 1:width, :]
    for d in range(1, width):
        part = shifted[(d - 1) * ts:d * ts]
        top = part[0:HALO_ROWS] + carried[(d - 1) * HALO_ROWS:d * HALO_ROWS]
        part = jnp.concatenate([top, part[HALO_ROWS:]], axis=0)
        acc = acc + part * w_ref[width - 1 - d:width - d, :]
    halo_ref[...] = x[ts - HALO_ROWS:]
    return acc


def _gdn_chunk_kernel(q_ref, k_ref, v_ref, z_ref, wq_ref, wk_ref, wv_ref, gcc_ref, bc_ref, gcr_ref,
                      onorm_ref, o_ref, hq_ref, hk_ref, hv_ref, inner_ref, edge_ref, state_ref, *,
                      heads_per_step):
    c = GDN_CHUNK
    dk = HEAD_DIM
    seq_step = pl.program_id(2)
    ts = q_ref.shape[1]

    @pl.when(seq_step == 0)
    def _():
        state_ref[...] = jnp.zeros(state_ref.shape, F32)
        for halo_ref in (hq_ref, hk_ref, hv_ref):
            halo_ref[...] = jnp.zeros(halo_ref.shape, BF16)
        inner_ref[...], edge_ref[...] = _shift_matrices(ts)

    yq = _silu(_causal_conv(q_ref, hq_ref, inner_ref, edge_ref, wq_ref))
    yk = _silu(_causal_conv(k_ref, hk_ref, inner_ref, edge_ref, wk_ref))
    yv = _silu(_causal_conv(v_ref, hv_ref, inner_ref, edge_ref, wv_ref))

    row = lax.broadcasted_iota(jnp.int32, (c, c), 0)
    col = lax.broadcasted_iota(jnp.int32, (c, c), 1)
    incl = row >= col
    strict = row > col
    eye = (row == col).astype(F32)
    lane = lax.broadcasted_iota(jnp.int32, (ts, LANES), 1)
    hb, nc = heads_per_step, ts // c

    per_head = []
    for i in range(hb):
        head = pl.program_id(1) * hb + i
        hl = slice(i * dk, (i + 1) * dk)
        qh, kh, vh = yq[:, hl], yk[:, hl], yv[:, hl]
        qh = qh * lax.rsqrt(jnp.sum(qh * qh, axis=-1, keepdims=True) + EPS) * (dk ** -0.5)
        kh = kh * lax.rsqrt(jnp.sum(kh * kh, axis=-1, keepdims=True) + EPS)
        pick = lane == head
        gcol = jnp.sum(jnp.where(pick, gcc_ref[0], 0.0), axis=-1, keepdims=True)
        bcol = jnp.sum(jnp.where(pick, bc_ref[0], 0.0), axis=-1, keepdims=True)
        grow = gcr_ref[0, pl.ds(head, 1), :]
        per_head.append((qh, kh, vh, gcol, bcol, grow))
    items = [(ci, i) for ci in range(nc) for i in range(hb)]
    qs, ks, vs, gcols, bcols, grows = [], [], [], [], [], []
    for ci, i in items:
        qh, kh, vh, gcol, bcol, grow = per_head[i]
        sl = slice(ci * c, (ci + 1) * c)
        qs.append(qh[sl])
        ks.append(kh[sl])
        vs.append(vh[sl])
        gcols.append(gcol[sl])
        bcols.append(bcol[sl])
        grows.append(grow[:, sl])
    n = len(items)
    rng = range(n)
    gammas = [jnp.where(incl, jnp.exp(jnp.where(incl, gcols[t] - grows[t], 0.0)), 0.0) for t in rng]
    kbs = [ks[t] * bcols[t] for t in rng]
    n_pows = [jnp.where(strict, -(_dot_nt(kbs[t], ks[t]) * gammas[t]), 0.0) for t in rng]
    t_invs = [eye + n_pows[t] for t in rng]
    for _ in range(5):
        n_pows = [_dot(n_pows[t], n_pows[t]) for t in rng]
        t_invs = [t_invs[t] + _dot(t_invs[t], n_pows[t]) for t in rng]
    e_gs = [jnp.exp(gcols[t]) for t in rng]
    wus = [_dot(t_invs[t], jnp.concatenate([kbs[t] * e_gs[t], vs[t] * bcols[t]], axis=1)) for t in rng]
    qks = [jnp.where(incl, _dot_nt(qs[t], ks[t]) * gammas[t], 0.0) for t in rng]
    g_lasts = [grows[t][:, c - 1:c] for t in rng]
    q_decs = [qs[t] * e_gs[t] for t in rng]
    k_decs = [ks[t] * jnp.exp(g_lasts[t] - gcols[t]) for t in rng]
    decays = [jnp.exp(g_lasts[t]) for t in rng]

    states = [state_ref[i] for i in range(hb)]
    outs = [None] * n
    for ci in range(nc):
        idx = [ci * hb + i for i in range(hb)]
        v_news = [wus[t][:, dk:] - _dot(wus[t][:, :dk], states[i]) for i, t in enumerate(idx)]
        o_state = [_dot(q_decs[t], states[i]) for i, t in enumerate(idx)]
        for i, t in enumerate(idx):
            outs[t] = o_state[i] + _dot(qks[t], v_news[i])
        states = [states[i] * decays[t] + _dot_tn(k_decs[t], v_news[i]) for i, t in enumerate(idx)]
    for i in range(hb):
        state_ref[i] = states[i]

    for t, (ci, i) in enumerate(items):
        o = outs[t]
        o = o * lax.rsqrt(jnp.mean(o * o, axis=-1, keepdims=True) + EPS) * onorm_ref[...]
        z = z_ref[0, ci * c:(ci + 1) * c, i * dk:(i + 1) * dk].astype(F32)
        o_ref[0, ci * c:(ci + 1) * c, i * dk:(i + 1) * dk] = (o * _silu(z)).astype(o_ref.dtype)


def gdn_chunk(proj, conv_w, gcc, betac, gcr, o_norm, *, n_heads, ts, heads_per_step):
    b, s, _ = proj.shape
    dk = HEAD_DIM
    ts = _tile(s, ts)
    hb = _tile(n_heads, heads_per_step)
    groups = n_heads // hb
    width = conv_w.shape[0]
    assert ts % GDN_CHUNK == 0 and width == GDN_CONV_TAPS and width - 1 <= HALO_ROWS <= ts
    tok = lambda part: pl.BlockSpec((1, ts, hb * dk), lambda bi, hi, si, p=part: (bi, si, hi + p * groups))
    cw = lambda part: pl.BlockSpec((width, hb * dk), lambda bi, hi, si, p=part: (0, hi + p * groups))
    gate_col = pl.BlockSpec((1, ts, LANES), lambda bi, hi, si: (bi, si, 0))
    return pl.pallas_call(
        functools.partial(_gdn_chunk_kernel, heads_per_step=hb),
        grid=(b, groups, s // ts),
        in_specs=[tok(0), tok(1), tok(2), tok(3), cw(0), cw(1), cw(2), gate_col, gate_col,
                  pl.BlockSpec((1, n_heads, ts), lambda bi, hi, si: (bi, 0, si)),
                  pl.BlockSpec((1, dk), lambda bi, hi, si: (0, 0))],
        out_specs=pl.BlockSpec((1, ts, hb * dk), lambda bi, hi, si: (bi, si, hi)),
        out_shape=jax.ShapeDtypeStruct((b, s, n_heads * dk), BF16),
        scratch_shapes=[pltpu.VMEM((HALO_ROWS, hb * dk), BF16)] * 3
                       + [pltpu.VMEM(((GDN_CONV_TAPS - 1) * ts, ts), BF16),
                          pltpu.VMEM(((GDN_CONV_TAPS - 1) * HALO_ROWS, HALO_ROWS), BF16),
                          pltpu.VMEM((hb, dk, dk), F32)],
        compiler_params=_params("parallel", "parallel", "arbitrary"),
        name="gdn_chunk",
    )(proj, proj, proj, proj, conv_w, conv_w, conv_w, gcc, betac, gcr,
      o_norm.reshape(1, dk).astype(F32))


def gdn_cast_weights(w_in, n_heads):
    wide = 4 * n_heads * HEAD_DIM
    return jnp.pad(w_in, ((0, 0), (0, 0), (0, wide + LANES - w_in.shape[2]))).astype(BF16)


def gated_deltanet(x, xb, ss, w_all, layer, conv_w, a_log, dt_bias, o_norm, w_out,
                   next_gain, batch):
    t, d = x.shape
    n_heads = a_log.shape[0]
    s = t // batch
    wide = 4 * n_heads * HEAD_DIM
    (proj,) = projection(xb, ss, w_all, layer, name="gdn_in_proj", col0=0, n_cols=wide, n_streams=1,
                         epilogue=_plain_epilogue, outs=[_row_col_tile(t, wide, BF16, 1024)], tn=1024)
    (ab,) = projection(xb, ss, w_all, layer, name="gdn_ab_proj", col0=wide, n_cols=LANES,
                       n_streams=1, epilogue=_plain_epilogue,
                       outs=[_row_col_tile(t, LANES, F32, LANES)], tn=LANES)
    gcc, betac = gdn_gates(ab, a_log, dt_bias, tb=512)
    gcc = gcc.reshape(batch, s, LANES)
    betac = betac.reshape(batch, s, LANES)
    gcr = jnp.swapaxes(gcc[:, :, :n_heads], 1, 2)
    o = gdn_chunk(proj.reshape(batch, s, wide), conv_w.astype(F32), gcc, betac, gcr, o_norm,
                  n_heads=n_heads, ts=256, heads_per_step=8)
    return matmul_residual(o.reshape(t, n_heads * HEAD_DIM), w_out, layer, x, next_gain,
                           name="gdn_out_proj", scale=1.0)


def _shortconv_epilogue(accs, extra_refs, out_refs, cols):
    gb, gc, xv = accs
    for rows, conv in _conv_row_chunks([gc, xv], extra_refs[0], cols):
        gate = gb[HALO_ROWS + rows.start:HALO_ROWS + rows.stop, :]
        out_refs[0][rows, cols] = (gate * conv).astype(out_refs[0].dtype)


def short_conv(x, xb, ss, w_in, layer, conv_w, w_out, next_gain, batch):
    t, d = x.shape
    tn = 256
    (y,) = projection(xb, ss, w_in, layer, name="sc_in_proj", col0=0, n_cols=d, n_streams=3,
                      epilogue=_shortconv_epilogue, outs=[_row_col_tile(t, d, BF16, tn)], tn=tn,
                      extras=[_col_vec(conv_w.astype(F32), tn)], halo=True, seq_len=t // batch)
    return matmul_residual(y, w_out, layer, x, next_gain, name="sc_out_proj", scale=1.0)


def _rope_table_kernel(pos_ref, freq_ref, cos_ref, sin_ref):
    ang = pos_ref[0].astype(F32) * freq_ref[...]
    lane = lax.broadcasted_iota(jnp.int32, ang.shape, 1)
    sign = jnp.where(lane < HEAD_DIM // 2, -1.0, 1.0)
    cos_ref[0] = jnp.cos(ang)
    sin_ref[0] = jnp.sin(ang) * sign


def rope_tables(positions, *, ts):
    b, s = positions.shape
    ts = _tile(s, ts)
    half = HEAD_DIM // 2
    inv_freq = ROPE_THETA ** (-jnp.arange(half, dtype=F32) / half)
    freq = jnp.concatenate([inv_freq, inv_freq]).reshape(1, HEAD_DIM)
    blk = pl.BlockSpec((1, ts, HEAD_DIM), lambda bi, si: (bi, si, 0))
    return pl.pallas_call(
        _rope_table_kernel,
        grid=(b, s // ts),
        in_specs=[pl.BlockSpec((1, ts, 1), lambda bi, si: (bi, si, 0)),
                  pl.BlockSpec((1, HEAD_DIM), lambda bi, si: (0, 0))],
        out_specs=[blk, blk],
        out_shape=[jax.ShapeDtypeStruct((b, s, HEAD_DIM), F32)] * 2,
        compiler_params=_params("parallel", "parallel"),
        name="rope_tables",
    )(positions.reshape(b, s, 1), freq)


def _norm_rope(seg, cos, sin, gain):
    seg = seg * lax.rsqrt(jnp.mean(seg * seg, axis=-1, keepdims=True) + EPS) * gain
    return seg * cos + pltpu.roll(seg, HEAD_DIM // 2, axis=1) * sin


def _moba_q_epilogue(accs, extra_refs, out_refs, cols):
    cos_ref, sin_ref, gain_ref = extra_refs
    cos, sin, gain = cos_ref[...], sin_ref[...], gain_ref[...]
    acc = accs[0]
    for _, hl in _heads(cols):
        seg = acc[:, hl.start - cols.start:hl.stop - cols.start]
        qh = _norm_rope(seg, cos, sin, gain) * (HEAD_DIM ** -0.5 * LOG2_E)
        out_refs[0][:, hl] = qh.astype(out_refs[0].dtype)


def _moba_k_epilogue(accs, extra_refs, out_refs, cols):
    cos_ref, sin_ref, gain_ref = extra_refs
    cos, sin, gain = cos_ref[...], sin_ref[...], gain_ref[...]
    ko_ref, km_ref = out_refs
    acc = accs[0]
    bs = MOBA_BLOCK
    for h, hl in _heads(cols):
        kh = _norm_rope(acc[:, hl.start - cols.start:hl.stop - cols.start], cos, sin, gain)
        for g in range(acc.shape[0] // bs):
            blk = kh[g * bs:(g + 1) * bs]
            ko_ref[0, h, g] = blk.astype(ko_ref.dtype)
            km_ref[g, :, hl] = jnp.mean(blk, axis=0, keepdims=True)


def _moba_v_epilogue(accs, extra_refs, out_refs, cols):
    vto_ref = out_refs[0]
    acc = accs[0]
    bs, dh = MOBA_BLOCK, HEAD_DIM
    sub = lax.broadcasted_iota(jnp.int32, (SUM_ROWS, bs), 0)
    ones_rows = jnp.where(sub == 0, 1.0, 0.0).astype(vto_ref.dtype)
    for h, hl in _heads(cols):
        for g in range(acc.shape[0] // bs):
            blk = acc[g * bs:(g + 1) * bs, hl.start - cols.start:hl.stop - cols.start]
            vto_ref[0, h, g, 0:dh, :] = blk.T.astype(vto_ref.dtype)
            vto_ref[0, h, g, dh:dh + SUM_ROWS, :] = ones_rows


def _moba_attn_kernel(q_ref, k_ref, vt_ref, km_ref, o_ref, bias_ref, *, group, slab, heads):
    bs, dh = MOBA_BLOCK, HEAD_DIM
    nb = km_ref.shape[2]
    qi = pl.program_id(2)
    hrange = range(heads)
    q = [q_ref[0, :, h * dh:(h + 1) * dh] for h in hrange]

    blk = lax.broadcasted_iota(jnp.int32, (nb, bs), 0)
    blk_f = blk.astype(F32)
    past = blk < qi
    for h in hrange:
        gate = functools.reduce(jnp.add, [_dot_nt(part, q[h]) for part in _split3(km_ref[0, h])])
        gate = jnp.where(past, gate, -jnp.inf)
        bias = jnp.full((nb, bs), -jnp.inf, F32)
        for _ in range(MOBA_TOPK):
            top = jnp.max(gate, axis=0, keepdims=True)
            first = jnp.min(jnp.where(gate == top, blk_f, float(nb)), axis=0, keepdims=True)
            hit = (blk_f == first) & (top > -jnp.inf)
            bias = jnp.where(hit, 0.0, bias)
            gate = jnp.where(hit, -jnp.inf, gate)
        bias_ref[h] = bias

    def slab_scores(h, j, n):
        return _dot_nt(k_ref[0, h, pl.ds(j, n)].reshape(n * bs, dh), q[h])

    def pieces(s_t):
        m_blk = jnp.max(s_t, axis=0, keepdims=True)
        return m_blk, jnp.exp2(s_t - m_blk).astype(BF16)

    def values(h, j, p):
        return jnp.dot(vt_ref[0, h, j], p, preferred_element_type=F32)[:dh + 8]

    def merge(m, acc, parts):
        m_new = functools.reduce(jnp.maximum, [m] + [mb for mb, _ in parts])
        acc = jnp.exp2(m - m_new) * acc
        for mb, ab in parts:
            acc = acc + jnp.exp2(mb - m_new) * ab
        return m_new, acc

    key_pos = lax.broadcasted_iota(jnp.int32, (bs, bs), 0)
    qry_pos = lax.broadcasted_iota(jnp.int32, (bs, bs), 1)
    own = [slab_scores(h, qi, 1) for h in hrange]
    own = [pieces(jnp.where(key_pos <= qry_pos, s_t, -jnp.inf)) for s_t in own]
    carry = tuple((own[h][0], values(h, qi, own[h][1])) for h in hrange)

    def body(it, carry):
        j0 = it * group
        slabs = [[slab_scores(h, j0 + u * slab, slab) for u in range(group // slab)] for h in hrange]
        mp = [[pieces(slabs[h][b // slab][(b % slab) * bs:(b % slab + 1) * bs]) for b in range(group)]
              for h in hrange]
        parts = [[(mp[h][b][0] + bias_ref[h, pl.ds(j0 + b, 1), :], values(h, j0 + b, mp[h][b][1]))
                  for b in range(group)] for h in hrange]
        return tuple(merge(*carry[h], parts[h]) for h in hrange)

    carry = lax.fori_loop(0, (qi + group - 1) // group, body, carry)
    for h in hrange:
        acc = carry[h][1]
        o_ref[0, :, h * dh:(h + 1) * dh] = (acc[:dh] / acc[dh:dh + 1]).T.astype(o_ref.dtype)


def moba_attn(q, k, vt, kmean):
    b, s, d = q.shape
    _, n_heads, nb, bs, dh = k.shape
    group = _tile(nb, 8)
    slab = _tile(group, 2)
    heads = _tile(n_heads, 4)
    return pl.pallas_call(
        functools.partial(_moba_attn_kernel, group=group, slab=slab, heads=heads),
        grid=(b, n_heads // heads, nb),
        in_specs=[pl.BlockSpec((1, bs, heads * dh), lambda bi, hi, qi: (bi, qi, hi)),
                  pl.BlockSpec((1, heads, nb, bs, dh), lambda bi, hi, qi: (bi, hi, 0, 0, 0)),
                  pl.BlockSpec((1, heads, nb, dh + SUM_ROWS, bs), lambda bi, hi, qi: (bi, hi, 0, 0, 0)),
                  pl.BlockSpec((1, heads, nb, dh), lambda bi, hi, qi: (bi, hi, 0, 0))],
        out_specs=pl.BlockSpec((1, bs, heads * dh), lambda bi, hi, qi: (bi, qi, hi)),
        out_shape=jax.ShapeDtypeStruct((b, s, d), BF16),
        scratch_shapes=[pltpu.VMEM((heads, nb, bs), F32)],
        compiler_params=_params("parallel", "parallel", "arbitrary"),
        name="moba_attn",
    )(q, k, vt, kmean)


def moba_attention(x, xb, ss, positions, w_in, layer, q_norm, k_norm, w_out, next_gain):
    t, d = x.shape
    batch, s = positions.shape
    n_heads = d // HEAD_DIM
    dh, bs = HEAD_DIM, MOBA_BLOCK
    assert s % bs == 0
    nb = s // bs
    tm = _tile(t, ROW_TILE)
    assert tm % bs == 0 and s % tm == 0
    tn = _tile(d, 512)
    hpt, bpt, tiles_per_seq = tn // dh, tm // bs, s // tm
    cos, sin = rope_tables(positions, ts=512)
    rows = pl.BlockSpec((tm, dh), lambda i, j: (i, 0))
    vec = pl.BlockSpec((1, dh), lambda i, j: (0, 0))
    rope_extras = lambda gain: [(cos.reshape(t, dh), rows), (sin.reshape(t, dh), rows),
                                (gain.reshape(1, dh).astype(F32), vec)]
    blocked = lambda i, j: (i // tiles_per_seq, j, i % tiles_per_seq, 0, 0)
    (q,) = projection(xb, ss, w_in, layer, name="moba_q_proj", col0=0, n_cols=d, n_streams=1,
                      epilogue=_moba_q_epilogue, outs=[_row_col_tile(t, d, BF16, tn)], tn=tn,
                      extras=rope_extras(q_norm), lag=True)
    k, kmean = projection(
        xb, ss, w_in, layer, name="moba_k_proj", col0=d, n_cols=d, n_streams=1,
        epilogue=_moba_k_epilogue, tn=tn, extras=rope_extras(k_norm), lag=True,
        outs=[(jax.ShapeDtypeStruct((batch, n_heads, nb, bs, dh), BF16),
               pl.BlockSpec((1, hpt, bpt, bs, dh), blocked)),
              (jax.ShapeDtypeStruct((t // bs, 1, d), F32),
               pl.BlockSpec((bpt, 1, tn), lambda i, j: (i, 0, j)))])
    (vt,) = projection(
        xb, ss, w_in, layer, name="moba_v_proj", col0=2 * d, n_cols=d, n_streams=1,
        epilogue=_moba_v_epilogue, tn=tn,
        outs=[(jax.ShapeDtypeStruct((batch, n_heads, nb, dh + SUM_ROWS, bs), BF16),
               pl.BlockSpec((1, hpt, bpt, dh + SUM_ROWS, bs), blocked))])
    kmean = kmean.reshape(batch, nb, n_heads, dh).transpose(0, 2, 1, 3)
    o = moba_attn(q.reshape(batch, s, d), k, vt, kmean)
    return matmul_residual(o.reshape(t, d), w_out, layer, x, next_gain, name="moba_out_proj", scale=1.0)


def swiglu_half_step(x, xb, ss, w_in, w_out, layer, next_gain):
    t = x.shape[0]
    f = w_in.shape[2] // 2
    tn = 512
    (act,) = projection(xb, ss, w_in, layer, name="ffn_in", col0=0, n_cols=f, n_streams=2,
                        epilogue=_swiglu_epilogue, outs=[_row_col_tile(t, f, BF16, tn)], tn=tn)
    return matmul_residual(act, w_out, layer, x, next_gain, name="ffn_out", scale=0.5)


def kernel(x, positions, norm_ffn1, norm_mix, norm_ffn2, ffn1_w_in, ffn1_w_out, ffn2_w_in, ffn2_w_out, gdn_w_in, gdn_conv_w, gdn_a_log, gdn_dt_bias, gdn_out_norm, gdn_w_out, sc_w_in, sc_conv_w, sc_w_out, moba_w_in, moba_q_norm, moba_k_norm, moba_w_out):
    batch, s, d = x.shape
    depth = norm_ffn1.shape[0]
    ffn1_w_in, ffn1_w_out, ffn2_w_in, ffn2_w_out, gdn_w_out, sc_w_in, sc_w_out, moba_w_in, moba_w_out = (
        w.astype(BF16) for w in (ffn1_w_in, ffn1_w_out, ffn2_w_in, ffn2_w_out, gdn_w_out, sc_w_in,
                                 sc_w_out, moba_w_in, moba_w_out))
    gdn_w_all = gdn_cast_weights(gdn_w_in, gdn_a_log.shape[1])
    xt = x.reshape(batch * s, d)
    xb, ss = scale_rows(xt, norm_ffn1[0], tm=256)
    for i in range(depth):
        kind, j = i % 3, i // 3
        after_layer = norm_ffn1[i + 1] if i + 1 < depth else None
        xt, xb, ss = swiglu_half_step(xt, xb, ss, ffn1_w_in, ffn1_w_out, i, norm_mix[i])
        if kind == 0:
            xt, xb, ss = gated_deltanet(xt, xb, ss, gdn_w_all, j, gdn_conv_w[j], gdn_a_log[j],
                                        gdn_dt_bias[j], gdn_out_norm[j], gdn_w_out, norm_ffn2[i], batch)
        elif kind == 1:
            xt, xb, ss = short_conv(xt, xb, ss, sc_w_in, j, sc_conv_w[j], sc_w_out, norm_ffn2[i], batch)
        else:
            xt, xb, ss = moba_attention(xt, xb, ss, positions, moba_w_in, j, moba_q_norm[j],
                                        moba_k_norm[j], moba_w_out, norm_ffn2[i])
        xt, xb, ss = swiglu_half_step(xt, xb, ss, ffn2_w_in, ffn2_w_out, i, after_layer)
    return xt.reshape(batch, s, d)
```

```python
import functools

import jax
import jax.numpy as jnp
from jax import lax
from jax.experimental import pallas as pl
from jax.experimental.pallas import tpu as pltpu

F32 = jnp.float32
BF16 = jnp.bfloat16

EPS = 1e-6
HEAD_DIM = 128
GDN_CHUNK = 64
GDN_CONV_TAPS = 4
MOBA_BLOCK = 256
MOBA_TOPK = 3
ROPE_THETA = 10000.0
LOG2_E = 1.4426950408889634
SUM_ROWS = 16

LANES = 128
MXU_COLS = 256
HALO_ROWS = 16
VMEM_LIMIT = 56 * 1024 * 1024
ROW_TILE = 1024
RES_ROW_TILE = 1024
RES_COL_TILE = 512
CONV_ROW_CHUNK = 128


def _tile(dim, want):
    t = min(dim, want)
    while dim % t:
        t //= 2
    return t


def _params(*sem):
    return pltpu.CompilerParams(dimension_semantics=sem, vmem_limit_bytes=VMEM_LIMIT)


def _dot(a, b):
    return jnp.dot(a.astype(BF16), b.astype(BF16), preferred_element_type=F32)


def _dot_nt(a, b):
    return lax.dot_general(a.astype(BF16), b.astype(BF16), (((1,), (1,)), ((), ())),
                           preferred_element_type=F32)


def _dot_tn(a, b):
    return lax.dot_general(a.astype(BF16), b.astype(BF16), (((0,), (0,)), ((), ())),
                           preferred_element_type=F32)


def _split3(x):
    hi = x.astype(BF16)
    r1 = x - hi.astype(F32)
    mid = r1.astype(BF16)
    lo = (r1 - mid.astype(F32)).astype(BF16)
    return hi, mid, lo


def _silu(x):
    return x * jax.nn.sigmoid(x)


def _fold_lanes(x):
    parts = [x[:, c * LANES:(c + 1) * LANES] for c in range(x.shape[1] // LANES)]
    return functools.reduce(jnp.add, parts)


def _scale_rows_kernel(x_ref, g_ref, xb_ref, ss_ref):
    x = x_ref[...]
    xb_ref[...] = (x * g_ref[...]).astype(BF16)
    ss_ref[...] = _fold_lanes(x * x)


def scale_rows(x, gain, *, tm):
    t, d = x.shape
    tm = _tile(t, tm)
    return pl.pallas_call(
        _scale_rows_kernel,
        grid=(t // tm,),
        in_specs=[pl.BlockSpec((tm, d), lambda i: (i, 0)), pl.BlockSpec((1, d), lambda i: (0, 0))],
        out_specs=[pl.BlockSpec((tm, d), lambda i: (i, 0)), pl.BlockSpec((tm, LANES), lambda i: (i, 0))],
        out_shape=[jax.ShapeDtypeStruct((t, d), BF16), jax.ShapeDtypeStruct((t, LANES), F32)],
        compiler_params=_params("parallel"),
        name="scale_rows",
    )(x, gain.reshape(1, d).astype(F32))


def _mm_residual_kernel(*refs, scale, emit_next):
    if emit_next:
        a_ref, w_ref, x_ref, g_ref, o_ref, xb_ref, ss_ref = refs
    else:
        a_ref, w_ref, x_ref, o_ref = refs
    xn = x_ref[...] + scale * jnp.dot(a_ref[...], w_ref[...], preferred_element_type=F32)
    o_ref[...] = xn
    if emit_next:
        xb_ref[...] = (xn * g_ref[...]).astype(BF16)
        part = _fold_lanes(xn * xn)

        @pl.when(pl.program_id(1) == 0)
        def _():
            ss_ref[...] = part

        @pl.when(pl.program_id(1) != 0)
        def _():
            ss_ref[...] += part


def matmul_residual(a, w, layer, x, next_gain, *, name, scale):
    t, k = a.shape
    n = w.shape[2]
    tm, tn = _tile(t, RES_ROW_TILE), _tile(n, RES_COL_TILE)
    emit_next = next_gain is not None
    tile = pl.BlockSpec((tm, tn), lambda i, j: (i, j))
    in_specs = [pl.BlockSpec((tm, k), lambda i, j: (i, 0)),
                pl.BlockSpec((pl.Squeezed(), k, tn), lambda i, j: (layer, 0, j)),
                tile]
    args = [a, w, x]
    out_specs, out_shape = [tile], [jax.ShapeDtypeStruct((t, n), F32)]
    if emit_next:
        in_specs.append(pl.BlockSpec((1, tn), lambda i, j: (0, j)))
        args.append(next_gain.reshape(1, n).astype(F32))
        out_specs += [tile, pl.BlockSpec((tm, LANES), lambda i, j: (i, 0))]
        out_shape += [jax.ShapeDtypeStruct((t, n), BF16), jax.ShapeDtypeStruct((t, LANES), F32)]
    outs = pl.pallas_call(
        functools.partial(_mm_residual_kernel, scale=scale, emit_next=emit_next),
        grid=(t // tm, n // tn),
        in_specs=in_specs,
        out_specs=out_specs,
        out_shape=out_shape,
        compiler_params=_params("parallel", "arbitrary"),
        name=name,
    )(*args)
    return tuple(outs) if emit_next else (outs[0], None, None)


def _proj_kernel(*refs, n_streams, n_extra, n_out, epilogue, halo, lag, seq_len, k_dim, nj, n_tiles):
    xb_ref, ss_ref = refs[0], refs[1]
    pos = 2
    if halo:
        xh_ref, sh_ref = refs[2], refs[3]
        pos = 4
    w_refs = refs[pos:pos + n_streams]
    pos += n_streams
    extra_refs = refs[pos:pos + n_extra]
    pos += n_extra
    out_refs = refs[pos:pos + n_out]
    pos += n_out
    scratch = list(refs[pos:])
    r_ref = scratch.pop(0)
    a_ref = scratch.pop(0) if halo else None
    tm = xb_ref.shape[0]
    tn = w_refs[0].shape[1]
    hr = HALO_ROWS if halo else 0
    step = pl.program_id(0)
    tile = jnp.minimum(step, n_tiles - 1)

    def row_factor(ss):
        return lax.rsqrt(jnp.sum(ss, axis=-1, keepdims=True) * (1.0 / k_dim) + EPS)

    seq_start = ((tile // nj) * tm) % seq_len == 0 if halo else None

    @pl.when(tile % nj == 0)
    def _():
        r_ref[hr:hr + tm, :] = row_factor(ss_ref[...])
        if halo:
            a_ref[hr:hr + tm, :] = xb_ref[...]
            a_ref[0:hr, :] = xh_ref[...]
            r_ref[0:hr, :] = jnp.where(seq_start, 0.0, row_factor(sh_ref[...]))

    def products(cols):
        a = a_ref[...] if halo else xb_ref[...]
        r = r_ref[...]
        return [jnp.dot(a, w[:, cols], preferred_element_type=F32) * r for w in w_refs]

    if not lag:
        strip = min(tn, MXU_COLS)
        for c0 in range(0, tn, strip):
            cols = slice(c0, c0 + strip)
            epilogue(products(cols), extra_refs, out_refs, cols)
        return

    slots = [scratch[:n_streams], scratch[n_streams:]]
    everything = slice(0, tn)

    @pl.when(step == 0)
    def _():
        for acc_ref in slots[1]:
            acc_ref[...] = jnp.zeros(acc_ref.shape, F32)

    def phase(cur, prev):
        epilogue(slots[prev], extra_refs, out_refs, everything)
        for acc_ref, acc in zip(slots[cur], products(everything)):
            acc_ref[...] = acc

    @pl.when(step % 2 == 0)
    def _():
        phase(0, 1)

    @pl.when(step % 2 == 1)
    def _():
        phase(1, 0)


def projection(xb, ss, w, layer, *, name, col0, n_cols, n_streams, epilogue, outs, tn,
               extras=(), halo=False, lag=False, seq_len=None):
    t, k = xb.shape
    tm, tn = _tile(t, ROW_TILE), _tile(n_cols, tn)
    assert col0 % tn == 0
    nj, j0 = n_cols // tn, col0 // tn
    n_tiles = (t // tm) * nj
    hr = HALO_ROWS if halo else 0

    def cur(step):
        tile = jnp.minimum(step, n_tiles - 1)
        return tile // nj, tile % nj

    def out_tile(step):
        tile = jnp.maximum(step - 1, 0) if lag else step
        return tile // nj, tile % nj

    on_cur = lambda f: (lambda step: f(*cur(step)))
    on_out = lambda spec: pl.BlockSpec(spec.block_shape, lambda step, f=spec.index_map: f(*out_tile(step)))
    rows_mode = pl.Buffered(1) if (halo or tn >= 1024) else None
    in_specs = [pl.BlockSpec((tm, k), on_cur(lambda i, j: (i, 0)), pipeline_mode=rows_mode),
                pl.BlockSpec((tm, LANES), on_cur(lambda i, j: (i, 0)))]
    args = [xb, ss]
    scratch = [pltpu.VMEM((tm + hr, 1), F32)]
    if halo:
        assert seq_len % tm == 0 and tm % HALO_ROWS == 0
        per = tm // HALO_ROWS
        prev = on_cur(lambda i, j: (jnp.maximum(i * per - 1, 0), 0))
        in_specs += [pl.BlockSpec((HALO_ROWS, k), prev), pl.BlockSpec((HALO_ROWS, LANES), prev)]
        args += [xb, ss]
        scratch.append(pltpu.VMEM((tm + HALO_ROWS, k), BF16))
    if lag:
        scratch += [pltpu.VMEM((tm + hr, tn), F32)] * (2 * n_streams)
    for s in range(n_streams):
        in_specs.append(pl.BlockSpec((pl.Squeezed(), k, tn),
                                     on_cur(lambda i, j, s=s: (layer, 0, j0 + j + s * nj))))
        args.append(w)
    for arr, spec in extras:
        in_specs.append(on_out(spec))
        args.append(arr)
    body = functools.partial(_proj_kernel, n_streams=n_streams, n_extra=len(extras), n_out=len(outs),
                             epilogue=epilogue, halo=halo, lag=lag, seq_len=seq_len, k_dim=k, nj=nj,
                             n_tiles=n_tiles)
    return pl.pallas_call(
        body,
        grid=(n_tiles + int(lag),),
        in_specs=in_specs,
        out_specs=[on_out(spec) for _, spec in outs],
        out_shape=[shape for shape, _ in outs],
        scratch_shapes=scratch,
        compiler_params=_params("arbitrary"),
        name=name,
    )(*args)


def _row_col_tile(t, n, dtype, tn):
    tm, tn = _tile(t, ROW_TILE), _tile(n, tn)
    return jax.ShapeDtypeStruct((t, n), dtype), pl.BlockSpec((tm, tn), lambda i, j: (i, j))


def _col_vec(v, tn):
    return v, pl.BlockSpec((v.shape[0], tn), lambda i, j: (0, j))


def _conv_row_chunks(exts, w_ref, cols, row_chunk=CONV_ROW_CHUNK):
    width = w_ref.shape[0]
    tm = exts[0].shape[0] - HALO_ROWS
    chunk = _tile(tm, row_chunk)
    for r0 in range(0, tm, chunk):
        acc = None
        for tap in range(width):
            start = HALO_ROWS - (width - 1) + tap + r0
            term = functools.reduce(jnp.multiply, [e[start:start + chunk, :] for e in exts])
            term = term * w_ref[tap:tap + 1, cols]
            acc = term if acc is None else acc + term
        yield slice(r0, r0 + chunk), acc


def _heads(cols):
    return [(c // HEAD_DIM, slice(c, c + HEAD_DIM)) for c in range(cols.start, cols.stop, HEAD_DIM)]


def _swiglu_epilogue(accs, extra_refs, out_refs, cols):
    gate, up = accs
    out_refs[0][:, cols] = (_silu(gate) * up).astype(out_refs[0].dtype)


def _plain_epilogue(accs, extra_refs, out_refs, cols):
    out_refs[0][:, cols] = accs[0].astype(out_refs[0].dtype)


def _gdn_gates_kernel(ab_ref, alog_ref, dtb_ref, gc_ref, beta_ref, *, n_heads):
    c = GDN_CHUNK
    xs = ab_ref[...] + dtb_ref[...]
    softplus = jnp.maximum(xs, 0.0) + jnp.log(1.0 + jnp.exp(-jnp.abs(xs)))
    g = -jnp.exp(alog_ref[...]) * softplus
    row = lax.broadcasted_iota(jnp.int32, (c, c), 0)
    col = lax.broadcasted_iota(jnp.int32, (c, c), 1)
    tri = (row >= col).astype(BF16)
    for r in range(0, g.shape[0], c):
        hi, mid, lo = _split3(g[r:r + c, :])
        gc_ref[r:r + c, :] = (jnp.dot(tri, hi, preferred_element_type=F32)
                              + jnp.dot(tri, mid, preferred_element_type=F32)
                              + jnp.dot(tri, lo, preferred_element_type=F32))
    beta_ref[...] = jax.nn.sigmoid(pltpu.roll(ab_ref[...], LANES - n_heads, axis=1))


def gdn_gates(ab, a_log, dt_bias, *, tb):
    t = ab.shape[0]
    h = a_log.shape[0]
    assert 2 * h <= LANES
    tb = _tile(t, tb)
    pad = lambda v: jnp.pad(v.astype(F32), (0, LANES - h)).reshape(1, LANES)
    row_blk = pl.BlockSpec((tb, LANES), lambda i: (i, 0))
    vec = pl.BlockSpec((1, LANES), lambda i: (0, 0))
    return pl.pallas_call(
        functools.partial(_gdn_gates_kernel, n_heads=h),
        grid=(t // tb,),
        in_specs=[row_blk, vec, vec],
        out_specs=[row_blk, row_blk],
        out_shape=[jax.ShapeDtypeStruct((t, LANES), F32)] * 2,
        compiler_params=_params("parallel"),
        name="gdn_gates",
    )(ab, pad(a_log), pad(dt_bias))


def _shift_matrices(ts):
    n = GDN_CONV_TAPS - 1
    r = lax.broadcasted_iota(jnp.int32, (n * ts, ts), 0)
    c = lax.broadcasted_iota(jnp.int32, (n * ts, ts), 1)
    inner = (c + r // ts + 1 == r % ts).astype(BF16)
    r = lax.broadcasted_iota(jnp.int32, (n * HALO_ROWS, HALO_ROWS), 0)
    c = lax.broadcasted_iota(jnp.int32, (n * HALO_ROWS, HALO_ROWS), 1)
    edge = (c == HALO_ROWS - (r // HALO_ROWS + 1) + r % HALO_ROWS).astype(BF16)
    return inner, edge


def _causal_conv(x_ref, halo_ref, inner_ref, edge_ref, w_ref):
    x = x_ref[0]
    ts = x.shape[0]
    width = w_ref.shape[0]
    shifted = jnp.dot(inner_ref[...], x, preferred_element_type=F32)
    carried = jnp.dot(edge_ref[...], halo_ref[...], preferred_element_type=F32)
    acc = x.astype(F32) * w_ref[width - 1:width, :]
    for d in range(1, width):
        part = shifted[(d - 1) * ts:d * ts]
        top = part[0:HALO_ROWS] + carried[(d - 1) * HALO_ROWS:d * HALO_ROWS]
        part = jnp.concatenate([top, part[HALO_ROWS:]], axis=0)
        acc = acc + part * w_ref[width - 1 - d:width - d, :]
    halo_ref[...] = x[ts - HALO_ROWS:]
    return acc


def _gdn_chunk_kernel(q_ref, k_ref, v_ref, z_ref, wq_ref, wk_ref, wv_ref, gcc_ref, bc_ref, gcr_ref,
                      onorm_ref, o_ref, hq_ref, hk_ref, hv_ref, inner_ref, edge_ref, state_ref, *,
                      heads_per_step):
    c = GDN_CHUNK
    dk = HEAD_DIM
    seq_step = pl.program_id(2)
    ts = q_ref.shape[1]

    @pl.when(seq_step == 0)
    def _():
        state_ref[...] = jnp.zeros(state_ref.shape, F32)
        for halo_ref in (hq_ref, hk_ref, hv_ref):
            halo_ref[...] = jnp.zeros(halo_ref.shape, BF16)
        inner_ref[...], edge_ref[...] = _shift_matrices(ts)

    yq = _silu(_causal_conv(q_ref, hq_ref, inner_ref, edge_ref, wq_ref))
    yk = _silu(_causal_conv(k_ref, hk_ref, inner_ref, edge_ref, wk_ref))
    yv = _silu(_causal_conv(v_ref, hv_ref, inner_ref, edge_ref, wv_ref))

    row = lax.broadcasted_iota(jnp.int32, (c, c), 0)
    col = lax.broadcasted_iota(jnp.int32, (c, c), 1)
    incl = row >= col
    strict = row > col
    eye = (row == col).astype(F32)
    lane = lax.broadcasted_iota(jnp.int32, (ts, LANES), 1)
    hb, nc = heads_per_step, ts // c

    per_head = []
    for i in range(hb):
        head = pl.program_id(1) * hb + i
        hl = slice(i * dk, (i + 1) * dk)
        qh, kh, vh = yq[:, hl], yk[:, hl], yv[:, hl]
        qh = qh * lax.rsqrt(jnp.sum(qh * qh, axis=-1, keepdims=True) + EPS) * (dk ** -0.5)
        kh = kh * lax.rsqrt(jnp.sum(kh * kh, axis=-1, keepdims=True) + EPS)
        pick = lane == head
        gcol = jnp.sum(jnp.where(pick, gcc_ref[0], 0.0), axis=-1, keepdims=True)
        bcol = jnp.sum(jnp.where(pick, bc_ref[0], 0.0), axis=-1, keepdims=True)
        grow = gcr_ref[0, pl.ds(head, 1), :]
        per_head.append((qh, kh, vh, gcol, bcol, grow))
    items = [(ci, i) for ci in range(nc) for i in range(hb)]
    qs, ks, vs, gcols, bcols, grows = [], [], [], [], [], []
    for ci, i in items:
        qh, kh, vh, gcol, bcol, grow = per_head[i]
        sl = slice(ci * c, (ci + 1) * c)
        qs.append(qh[sl])
        ks.append(kh[sl])
        vs.append(vh[sl])
        gcols.append(gcol[sl])
        bcols.append(bcol[sl])
        grows.append(grow[:, sl])
    n = len(items)
    rng = range(n)
    gammas = [jnp.where(incl, jnp.exp(jnp.where(incl, gcols[t] - grows[t], 0.0)), 0.0) for t in rng]
    kbs = [ks[t] * bcols[t] for t in rng]
    n_pows = [jnp.where(strict, -(_dot_nt(kbs[t], ks[t]) * gammas[t]), 0.0) for t in rng]
    t_invs = [eye + n_pows[t] for t in rng]
    for _ in range(5):
        n_pows = [_dot(n_pows[t], n_pows[t]) for t in rng]
        t_invs = [t_invs[t] + _dot(t_invs[t], n_pows[t]) for t in rng]
    e_gs = [jnp.exp(gcols[t]) for t in rng]
    wus = [_dot(t_invs[t], jnp.concatenate([kbs[t] * e_gs[t], vs[t] * bcols[t]], axis=1)) for t in rng]
    qks = [jnp.where(incl, _dot_nt(qs[t], ks[t]) * gammas[t], 0.0) for t in rng]
    g_lasts = [grows[t][:, c - 1:c] for t in rng]
    q_decs = [qs[t] * e_gs[t] for t in rng]
    k_decs = [ks[t] * jnp.exp(g_lasts[t] - gcols[t]) for t in rng]
    decays = [jnp.exp(g_lasts[t]) for t in rng]

    states = [state_ref[i] for i in range(hb)]
    outs = [None] * n
    for ci in range(nc):
        idx = [ci * hb + i for i in range(hb)]
        v_news = [wus[t][:, dk:] - _dot(wus[t][:, :dk], states[i]) for i, t in enumerate(idx)]
        o_state = [_dot(q_decs[t], states[i]) for i, t in enumerate(idx)]
        for i, t in enumerate(idx):
            outs[t] = o_state[i] + _dot(qks[t], v_news[i])
        states = [states[i] * decays[t] + _dot_tn(k_decs[t], v_news[i]) for i, t in enumerate(idx)]
    for i in range(hb):
        state_ref[i] = states[i]

    for t, (ci, i) in enumerate(items):
        o = outs[t]
        o = o * lax.rsqrt(jnp.mean(o * o, axis=-1, keepdims=True) + EPS) * onorm_ref[...]
        z = z_ref[0, ci * c:(ci + 1) * c, i * dk:(i + 1) * dk].astype(F32)
        o_ref[0, ci * c:(ci + 1) * c, i * dk:(i + 1) * dk] = (o * _silu(z)).astype(o_ref.dtype)


def gdn_chunk(proj, conv_w, gcc, betac, gcr, o_norm, *, n_heads, ts, heads_per_step):
    b, s, _ = proj.shape
    dk = HEAD_DIM
    ts = _tile(s, ts)
    hb = _tile(n_heads, heads_per_step)
    groups = n_heads // hb
    width = conv_w.shape[0]
    assert ts % GDN_CHUNK == 0 and width == GDN_CONV_TAPS and width - 1 <= HALO_ROWS <= ts
    tok = lambda part: pl.BlockSpec((1, ts, hb * dk), lambda bi, hi, si, p=part: (bi, si, hi + p * groups))
    cw = lambda part: pl.BlockSpec((width, hb * dk), lambda bi, hi, si, p=part: (0, hi + p * groups))
    gate_col = pl.BlockSpec((1, ts, LANES), lambda bi, hi, si: (bi, si, 0))
    return pl.pallas_call(
        functools.partial(_gdn_chunk_kernel, heads_per_step=hb),
        grid=(b, groups, s // ts),
        in_specs=[tok(0), tok(1), tok(2), tok(3), cw(0), cw(1), cw(2), gate_col, gate_col,
                  pl.BlockSpec((1, n_heads, ts), lambda bi, hi, si: (bi, 0, si)),
                  pl.BlockSpec((1, dk), lambda bi, hi, si: (0, 0))],
        out_specs=pl.BlockSpec((1, ts, hb * dk), lambda bi, hi, si: (bi, si, hi)),
        out_shape=jax.ShapeDtypeStruct((b, s, n_heads * dk), BF16),
        scratch_shapes=[pltpu.VMEM((HALO_ROWS, hb * dk), BF16)] * 3
                       + [pltpu.VMEM(((GDN_CONV_TAPS - 1) * ts, ts), BF16),
                          pltpu.VMEM(((GDN_CONV_TAPS - 1) * HALO_ROWS, HALO_ROWS), BF16),
                          pltpu.VMEM((hb, dk, dk), F32)],
        compiler_params=_params("parallel", "parallel", "arbitrary"),
        name="gdn_chunk",
    )(proj, proj, proj, proj, conv_w, conv_w, conv_w, gcc, betac, gcr,
      o_norm.reshape(1, dk).astype(F32))


def gdn_cast_weights(w_in, n_heads):
    wide = 4 * n_heads * HEAD_DIM
    return jnp.pad(w_in, ((0, 0), (0, 0), (0, wide + LANES - w_in.shape[2]))).astype(BF16)


def gated_deltanet(x, xb, ss, w_all, layer, conv_w, a_log, dt_bias, o_norm, w_out,
                   next_gain, batch):
    t, d = x.shape
    n_heads = a_log.shape[0]
    s = t // batch
    wide = 4 * n_heads * HEAD_DIM
    (proj,) = projection(xb, ss, w_all, layer, name="gdn_in_proj", col0=0, n_cols=wide, n_streams=1,
                         epilogue=_plain_epilogue, outs=[_row_col_tile(t, wide, BF16, 1024)], tn=1024)
    (ab,) = projection(xb, ss, w_all, layer, name="gdn_ab_proj", col0=wide, n_cols=LANES,
                       n_streams=1, epilogue=_plain_epilogue,
                       outs=[_row_col_tile(t, LANES, F32, LANES)], tn=LANES)
    gcc, betac = gdn_gates(ab, a_log, dt_bias, tb=512)
    gcc = gcc.reshape(batch, s, LANES)
    betac = betac.reshape(batch, s, LANES)
    gcr = jnp.swapaxes(gcc[:, :, :n_heads], 1, 2)
    o = gdn_chunk(proj.reshape(batch, s, wide), conv_w.astype(F32), gcc, betac, gcr, o_norm,
                  n_heads=n_heads, ts=256, heads_per_step=16)
    return matmul_residual(o.reshape(t, n_heads * HEAD_DIM), w_out, layer, x, next_gain,
                           name="gdn_out_proj", scale=1.0)


def _shortconv_epilogue(accs, extra_refs, out_refs, cols):
    gb, gc, xv = accs
    for rows, conv in _conv_row_chunks([gc, xv], extra_refs[0], cols):
        gate = gb[HALO_ROWS + rows.start:HALO_ROWS + rows.stop, :]
        out_refs[0][rows, cols] = (gate * conv).astype(out_refs[0].dtype)


def short_conv(x, xb, ss, w_in, layer, conv_w, w_out, next_gain, batch):
    t, d = x.shape
    tn = 256
    (y,) = projection(xb, ss, w_in, layer, name="sc_in_proj", col0=0, n_cols=d, n_streams=3,
                      epilogue=_shortconv_epilogue, outs=[_row_col_tile(t, d, BF16, tn)], tn=tn,
                      extras=[_col_vec(conv_w.astype(F32), tn)], halo=True, seq_len=t // batch)
    return matmul_residual(y, w_out, layer, x, next_gain, name="sc_out_proj", scale=1.0)


def _rope_table_kernel(pos_ref, freq_ref, cos_ref, sin_ref):
    ang = pos_ref[0].astype(F32) * freq_ref[...]
    lane = lax.broadcasted_iota(jnp.int32, ang.shape, 1)
    sign = jnp.where(lane < HEAD_DIM // 2, -1.0, 1.0)
    cos_ref[0] = jnp.cos(ang)
    sin_ref[0] = jnp.sin(ang) * sign


def rope_tables(positions, *, ts):
    b, s = positions.shape
    ts = _tile(s, ts)
    half = HEAD_DIM // 2
    inv_freq = ROPE_THETA ** (-jnp.arange(half, dtype=F32) / half)
    freq = jnp.concatenate([inv_freq, inv_freq]).reshape(1, HEAD_DIM)
    blk = pl.BlockSpec((1, ts, HEAD_DIM), lambda bi, si: (bi, si, 0))
    return pl.pallas_call(
        _rope_table_kernel,
        grid=(b, s // ts),
        in_specs=[pl.BlockSpec((1, ts, 1), lambda bi, si: (bi, si, 0)),
                  pl.BlockSpec((1, HEAD_DIM), lambda bi, si: (0, 0))],
        out_specs=[blk, blk],
        out_shape=[jax.ShapeDtypeStruct((b, s, HEAD_DIM), F32)] * 2,
        compiler_params=_params("parallel", "parallel"),
        name="rope_tables",
    )(positions.reshape(b, s, 1), freq)


def _norm_rope(seg, cos, sin, gain):
    seg = seg * lax.rsqrt(jnp.mean(seg * seg, axis=-1, keepdims=True) + EPS) * gain
    return seg * cos + pltpu.roll(seg, HEAD_DIM // 2, axis=1) * sin


def _moba_q_epilogue(accs, extra_refs, out_refs, cols):
    cos_ref, sin_ref, gain_ref = extra_refs
    cos, sin, gain = cos_ref[...], sin_ref[...], gain_ref[...]
    acc = accs[0]
    for _, hl in _heads(cols):
        seg = acc[:, hl.start - cols.start:hl.stop - cols.start]
        qh = _norm_rope(seg, cos, sin, gain) * (HEAD_DIM ** -0.5 * LOG2_E)
        out_refs[0][:, hl] = qh.astype(out_refs[0].dtype)


def _moba_k_epilogue(accs, extra_refs, out_refs, cols):
    cos_ref, sin_ref, gain_ref = extra_refs
    cos, sin, gain = cos_ref[...], sin_ref[...], gain_ref[...]
    ko_ref, km_ref = out_refs
    acc = accs[0]
    bs = MOBA_BLOCK
    for h, hl in _heads(cols):
        kh = _norm_rope(acc[:, hl.start - cols.start:hl.stop - cols.start], cos, sin, gain)
        for g in range(acc.shape[0] // bs):
            blk = kh[g * bs:(g + 1) * bs]
            ko_ref[0, h, g] = blk.astype(ko_ref.dtype)
            km_ref[g, :, hl] = jnp.mean(blk, axis=0, keepdims=True)


def _moba_v_epilogue(accs, extra_refs, out_refs, cols):
    vto_ref = out_refs[0]
    acc = accs[0]
    bs, dh = MOBA_BLOCK, HEAD_DIM
    sub = lax.broadcasted_iota(jnp.int32, (SUM_ROWS, bs), 0)
    ones_rows = jnp.where(sub == 0, 1.0, 0.0).astype(vto_ref.dtype)
    for h, hl in _heads(cols):
        for g in range(acc.shape[0] // bs):
            blk = acc[g * bs:(g + 1) * bs, hl.start - cols.start:hl.stop - cols.start]
            vto_ref[0, h, g, 0:dh, :] = blk.T.astype(vto_ref.dtype)
            vto_ref[0, h, g, dh:dh + SUM_ROWS, :] = ones_rows


def _moba_attn_kernel(q_ref, k_ref, vt_ref, km_ref, o_ref, bias_ref, *, group, slab, heads):
    bs, dh = MOBA_BLOCK, HEAD_DIM
    nb = km_ref.shape[2]
    qi = pl.program_id(2)
    hrange = range(heads)
    q = [q_ref[0, :, h * dh:(h + 1) * dh] for h in hrange]

    blk = lax.broadcasted_iota(jnp.int32, (nb, bs), 0)
    blk_f = blk.astype(F32)
    past = blk < qi
    for h in hrange:
        gate = functools.reduce(jnp.add, [_dot_nt(part, q[h]) for part in _split3(km_ref[0, h])])
        gate = jnp.where(past, gate, -jnp.inf)
        bias = jnp.full((nb, bs), -jnp.inf, F32)
        for _ in range(MOBA_TOPK):
            top = jnp.max(gate, axis=0, keepdims=True)
            first = jnp.min(jnp.where(gate == top, blk_f, float(nb)), axis=0, keepdims=True)
            hit = (blk_f == first) & (top > -jnp.inf)
            bias = jnp.where(hit, 0.0, bias)
            gate = jnp.where(hit, -jnp.inf, gate)
        bias_ref[h] = bias

    def slab_scores(h, j, n):
        return _dot_nt(k_ref[0, h, pl.ds(j, n)].reshape(n * bs, dh), q[h])

    def pieces(s_t):
        m_blk = jnp.max(s_t, axis=0, keepdims=True)
        return m_blk, jnp.exp2(s_t - m_blk).astype(BF16)

    def values(h, j, p):
        return jnp.dot(vt_ref[0, h, j], p, preferred_element_type=F32)[:dh + 8]

    def merge(m, acc, parts):
        m_new = functools.reduce(jnp.maximum, [m] + [mb for mb, _ in parts])
        acc = jnp.exp2(m - m_new) * acc
        for mb, ab in parts:
            acc = acc + jnp.exp2(mb - m_new) * ab
        return m_new, acc

    key_pos = lax.broadcasted_iota(jnp.int32, (bs, bs), 0)
    qry_pos = lax.broadcasted_iota(jnp.int32, (bs, bs), 1)
    own = [slab_scores(h, qi, 1) for h in hrange]
    own = [pieces(jnp.where(key_pos <= qry_pos, s_t, -jnp.inf)) for s_t in own]
    carry = tuple((own[h][0], values(h, qi, own[h][1])) for h in hrange)

    def body(it, carry):
        j0 = it * group
        slabs = [[slab_scores(h, j0 + u * slab, slab) for u in range(group // slab)] for h in hrange]
        mp = [[pieces(slabs[h][b // slab][(b % slab) * bs:(b % slab + 1) * bs]) for b in range(group)]
              for h in hrange]
        parts = [[(mp[h][b][0] + bias_ref[h, pl.ds(j0 + b, 1), :], values(h, j0 + b, mp[h][b][1]))
                  for b in range(group)] for h in hrange]
        return tuple(merge(*carry[h], parts[h]) for h in hrange)

    carry = lax.fori_loop(0, (qi + group - 1) // group, body, carry)
    for h in hrange:
        acc = carry[h][1]
        o_ref[0, :, h * dh:(h + 1) * dh] = (acc[:dh] / acc[dh:dh + 1]).T.astype(o_ref.dtype)


def moba_attn(q, k, vt, kmean):
    b, s, d = q.shape
    _, n_heads, nb, bs, dh = k.shape
    group = _tile(nb, 8)
    slab = _tile(group, 2)
    heads = _tile(n_heads, 4)
    return pl.pallas_call(
        functools.partial(_moba_attn_kernel, group=group, slab=slab, heads=heads),
        grid=(b, n_heads // heads, nb),
        in_specs=[pl.BlockSpec((1, bs, heads * dh), lambda bi, hi, qi: (bi, qi, hi)),
                  pl.BlockSpec((1, heads, nb, bs, dh), lambda bi, hi, qi: (bi, hi, 0, 0, 0)),
                  pl.BlockSpec((1, heads, nb, dh + SUM_ROWS, bs), lambda bi, hi, qi: (bi, hi, 0, 0, 0)),
                  pl.BlockSpec((1, heads, nb, dh), lambda bi, hi, qi: (bi, hi, 0, 0))],
        out_specs=pl.BlockSpec((1, bs, heads * dh), lambda bi, hi, qi: (bi, qi, hi)),
        out_shape=jax.ShapeDtypeStruct((b, s, d), BF16),
        scratch_shapes=[pltpu.VMEM((heads, nb, bs), F32)],
        compiler_params=_params("parallel", "parallel", "arbitrary"),
        name="moba_attn",
    )(q, k, vt, kmean)


def moba_attention(x, xb, ss, positions, w_in, layer, q_norm, k_norm, w_out, next_gain):
    t, d = x.shape
    batch, s = positions.shape
    n_heads = d // HEAD_DIM
    dh, bs = HEAD_DIM, MOBA_BLOCK
    assert s % bs == 0
    nb = s // bs
    tm = _tile(t, ROW_TILE)
    assert tm % bs == 0 and s % tm == 0
    tn = _tile(d, 512)
    hpt, bpt, tiles_per_seq = tn // dh, tm // bs, s // tm
    cos, sin = rope_tables(positions, ts=512)
    rows = pl.BlockSpec((tm, dh), lambda i, j: (i, 0))
    vec = pl.BlockSpec((1, dh), lambda i, j: (0, 0))
    rope_extras = lambda gain: [(cos.reshape(t, dh), rows), (sin.reshape(t, dh), rows),
                                (gain.reshape(1, dh).astype(F32), vec)]
    blocked = lambda i, j: (i // tiles_per_seq, j, i % tiles_per_seq, 0, 0)
    (q,) = projection(xb, ss, w_in, layer, name="moba_q_proj", col0=0, n_cols=d, n_streams=1,
                      epilogue=_moba_q_epilogue, outs=[_row_col_tile(t, d, BF16, tn)], tn=tn,
                      extras=rope_extras(q_norm), lag=True)
    k, kmean = projection(
        xb, ss, w_in, layer, name="moba_k_proj", col0=d, n_cols=d, n_streams=1,
        epilogue=_moba_k_epilogue, tn=tn, extras=rope_extras(k_norm), lag=True,
        outs=[(jax.ShapeDtypeStruct((batch, n_heads, nb, bs, dh), BF16),
               pl.BlockSpec((1, hpt, bpt, bs, dh), blocked)),
              (jax.ShapeDtypeStruct((t // bs, 1, d), F32),
               pl.BlockSpec((bpt, 1, tn), lambda i, j: (i, 0, j)))])
    (vt,) = projection(
        xb, ss, w_in, layer, name="moba_v_proj", col0=2 * d, n_cols=d, n_streams=1,
        epilogue=_moba_v_epilogue, tn=tn, lag=True,
        outs=[(jax.ShapeDtypeStruct((batch, n_heads, nb, dh + SUM_ROWS, bs), BF16),
               pl.BlockSpec((1, hpt, bpt, dh + SUM_ROWS, bs), blocked))])
    kmean = kmean.reshape(batch, nb, n_heads, dh).transpose(0, 2, 1, 3)
    o = moba_attn(q.reshape(batch, s, d), k, vt, kmean)
    return matmul_residual(o.reshape(t, d), w_out, layer, x, next_gain, name="moba_out_proj", scale=1.0)


def swiglu_half_step(x, xb, ss, w_in, w_out, layer, next_gain):
    t = x.shape[0]
    f = w_in.shape[2] // 2
    tn = 512
    (act,) = projection(xb, ss, w_in, layer, name="ffn_in", col0=0, n_cols=f, n_streams=2,
                        epilogue=_swiglu_epilogue, outs=[_row_col_tile(t, f, BF16, tn)], tn=tn)
    return matmul_residual(act, w_out, layer, x, next_gain, name="ffn_out", scale=0.5)


def kernel(x, positions, norm_ffn1, norm_mix, norm_ffn2, ffn1_w_in, ffn1_w_out, ffn2_w_in, ffn2_w_out, gdn_w_in, gdn_conv_w, gdn_a_log, gdn_dt_bias, gdn_out_norm, gdn_w_out, sc_w_in, sc_conv_w, sc_w_out, moba_w_in, moba_q_norm, moba_k_norm, moba_w_out):
    batch, s, d = x.shape
    depth = norm_ffn1.shape[0]
    ffn1_w_in, ffn1_w_out, ffn2_w_in, ffn2_w_out, gdn_w_out, sc_w_in, sc_w_out, moba_w_in, moba_w_out = (
        w.astype(BF16) for w in (ffn1_w_in, ffn1_w_out, ffn2_w_in, ffn2_w_out, gdn_w_out, sc_w_in,
                                 sc_w_out, moba_w_in, moba_w_out))
    gdn_w_all = gdn_cast_weights(gdn_w_in, gdn_a_log.shape[1])
    xt = x.reshape(batch * s, d)
    xb, ss = scale_rows(xt, norm_ffn1[0], tm=256)
    for i in range(depth):
        kind, j = i % 3, i // 3
        after_layer = norm_ffn1[i + 1] if i + 1 < depth else None
        xt, xb, ss = swiglu_half_step(xt, xb, ss, ffn1_w_in, ffn1_w_out, i, norm_mix[i])
        if kind == 0:
            xt, xb, ss = gated_deltanet(xt, xb, ss, gdn_w_all, j, gdn_conv_w[j], gdn_a_log[j],
                                        gdn_dt_bias[j], gdn_out_norm[j], gdn_w_out, norm_ffn2[i], batch)
        elif kind == 1:
            xt, xb, ss = short_conv(xt, xb, ss, sc_w_in, j, sc_conv_w[j], sc_w_out, norm_ffn2[i], batch)
        else:
            xt, xb, ss = moba_attention(xt, xb, ss, positions, moba_w_in, j, moba_q_norm[j],
                                        moba_k_norm[j], moba_w_out, norm_ffn2[i])
        xt, xb, ss = swiglu_half_step(xt, xb, ss, ffn2_w_in, ffn2_w_out, i, after_layer)
    return xt.reshape(batch, s, d)
```

```python
import functools

import jax
import jax.numpy as jnp
from jax import lax
from jax.experimental import pallas as pl
from jax.experimental.pallas import tpu as pltpu

F32 = jnp.float32
BF16 = jnp.bfloat16

EPS = 1e-6
HEAD_DIM = 128
GDN_CHUNK = 64
GDN_CONV_TAPS = 4
MOBA_BLOCK = 256
MOBA_TOPK = 3
ROPE_THETA = 10000.0
LOG2_E = 1.4426950408889634
SUM_ROWS = 16

LANES = 128
MXU_COLS = 256
HALO_ROWS = 16
VMEM_LIMIT = 56 * 1024 * 1024
ROW_TILE = 1024
RES_ROW_TILE = 1024
RES_COL_TILE = 512
CONV_ROW_CHUNK = 128


def _tile(dim, want):
    t = min(dim, want)
    while dim % t:
        t //= 2
    return t


def _params(*sem):
    return pltpu.CompilerParams(dimension_semantics=sem, vmem_limit_bytes=VMEM_LIMIT)


def _dot(a, b):
    return jnp.dot(a.astype(BF16), b.astype(BF16), preferred_element_type=F32)


def _dot_nt(a, b):
    return lax.dot_general(a.astype(BF16), b.astype(BF16), (((1,), (1,)), ((), ())),
                           preferred_element_type=F32)


def _dot_tn(a, b):
    return lax.dot_general(a.astype(BF16), b.astype(BF16), (((0,), (0,)), ((), ())),
                           preferred_element_type=F32)


def _split3(x):
    hi = x.astype(BF16)
    r1 = x - hi.astype(F32)
    mid = r1.astype(BF16)
    lo = (r1 - mid.astype(F32)).astype(BF16)
    return hi, mid, lo


def _silu(x):
    return x * jax.nn.sigmoid(x)


def _fold_lanes(x):
    parts = [x[:, c * LANES:(c + 1) * LANES] for c in range(x.shape[1] // LANES)]
    return functools.reduce(jnp.add, parts)


def _scale_rows_kernel(x_ref, g_ref, xb_ref, ss_ref):
    x = x_ref[...]
    xb_ref[...] = (x * g_ref[...]).astype(BF16)
    ss_ref[...] = _fold_lanes(x * x)


def scale_rows(x, gain, *, tm):
    t, d = x.shape
    tm = _tile(t, tm)
    return pl.pallas_call(
        _scale_rows_kernel,
        grid=(t // tm,),
        in_specs=[pl.BlockSpec((tm, d), lambda i: (i, 0)), pl.BlockSpec((1, d), lambda i: (0, 0))],
        out_specs=[pl.BlockSpec((tm, d), lambda i: (i, 0)), pl.BlockSpec((tm, LANES), lambda i: (i, 0))],
        out_shape=[jax.ShapeDtypeStruct((t, d), BF16), jax.ShapeDtypeStruct((t, LANES), F32)],
        compiler_params=_params("parallel"),
        name="scale_rows",
    )(x, gain.reshape(1, d).astype(F32))


def _mm_residual_kernel(*refs, scale, emit_next):
    if emit_next:
        a_ref, w_ref, x_ref, g_ref, o_ref, xb_ref, ss_ref = refs
    else:
        a_ref, w_ref, x_ref, o_ref = refs
    xn = x_ref[...] + scale * jnp.dot(a_ref[...], w_ref[...], preferred_element_type=F32)
    o_ref[...] = xn
    if emit_next:
        xb_ref[...] = (xn * g_ref[...]).astype(BF16)
        part = _fold_lanes(xn * xn)

        @pl.when(pl.program_id(1) == 0)
        def _():
            ss_ref[...] = part

        @pl.when(pl.program_id(1) != 0)
        def _():
            ss_ref[...] += part


def matmul_residual(a, w, layer, x, next_gain, *, name, scale):
    t, k = a.shape
    n = w.shape[2]
    tm, tn = _tile(t, RES_ROW_TILE), _tile(n, RES_COL_TILE)
    emit_next = next_gain is not None
    tile = pl.BlockSpec((tm, tn), lambda i, j: (i, j))
    in_specs = [pl.BlockSpec((tm, k), lambda i, j: (i, 0)),
                pl.BlockSpec((pl.Squeezed(), k, tn), lambda i, j: (layer, 0, j)),
                tile]
    args = [a, w, x]
    out_specs, out_shape = [tile], [jax.ShapeDtypeStruct((t, n), F32)]
    if emit_next:
        in_specs.append(pl.BlockSpec((1, tn), lambda i, j: (0, j)))
        args.append(next_gain.reshape(1, n).astype(F32))
        out_specs += [tile, pl.BlockSpec((tm, LANES), lambda i, j: (i, 0))]
        out_shape += [jax.ShapeDtypeStruct((t, n), BF16), jax.ShapeDtypeStruct((t, LANES), F32)]
    outs = pl.pallas_call(
        functools.partial(_mm_residual_kernel, scale=scale, emit_next=emit_next),
        grid=(t // tm, n // tn),
        in_specs=in_specs,
        out_specs=out_specs,
        out_shape=out_shape,
        compiler_params=_params("parallel", "arbitrary"),
        name=name,
    )(*args)
    return tuple(outs) if emit_next else (outs[0], None, None)


def _proj_kernel(*refs, n_streams, n_extra, n_out, epilogue, halo, lag, seq_len, k_dim, nj, n_tiles):
    xb_ref, ss_ref = refs[0], refs[1]
    pos = 2
    if halo:
        xh_ref, sh_ref = refs[2], refs[3]
        pos = 4
    w_refs = refs[pos:pos + n_streams]
    pos += n_streams
    extra_refs = refs[pos:pos + n_extra]
    pos += n_extra
    out_refs = refs[pos:pos + n_out]
    pos += n_out
    scratch = list(refs[pos:])
    r_ref = scratch.pop(0)
    a_ref = scratch.pop(0) if halo else None
    tm = xb_ref.shape[0]
    tn = w_refs[0].shape[1]
    hr = HALO_ROWS if halo else 0
    step = pl.program_id(0)
    tile = jnp.minimum(step, n_tiles - 1)

    def row_factor(ss):
        return lax.rsqrt(jnp.sum(ss, axis=-1, keepdims=True) * (1.0 / k_dim) + EPS)

    seq_start = ((tile // nj) * tm) % seq_len == 0 if halo else None

    @pl.when(tile % nj == 0)
    def _():
        r_ref[hr:hr + tm, :] = row_factor(ss_ref[...])
        if halo:
            a_ref[hr:hr + tm, :] = xb_ref[...]
            a_ref[0:hr, :] = xh_ref[...]
            r_ref[0:hr, :] = jnp.where(seq_start, 0.0, row_factor(sh_ref[...]))

    def products(cols):
        a = a_ref[...] if halo else xb_ref[...]
        r = r_ref[...]
        return [jnp.dot(a, w[:, cols], preferred_element_type=F32) * r for w in w_refs]

    if not lag:
        strip = min(tn, MXU_COLS)
        for c0 in range(0, tn, strip):
            cols = slice(c0, c0 + strip)
            epilogue(products(cols), extra_refs, out_refs, cols)
        return

    slots = [scratch[:n_streams], scratch[n_streams:]]
    everything = slice(0, tn)

    @pl.when(step == 0)
    def _():
        for acc_ref in slots[1]:
            acc_ref[...] = jnp.zeros(acc_ref.shape, F32)

    def phase(cur, prev):
        epilogue(slots[prev], extra_refs, out_refs, everything)
        for acc_ref, acc in zip(slots[cur], products(everything)):
            acc_ref[...] = acc

    @pl.when(step % 2 == 0)
    def _():
        phase(0, 1)

    @pl.when(step % 2 == 1)
    def _():
        phase(1, 0)


def projection(xb, ss, w, layer, *, name, col0, n_cols, n_streams, epilogue, outs, tn,
               extras=(), halo=False, lag=False, seq_len=None):
    t, k = xb.shape
    tm, tn = _tile(t, ROW_TILE), _tile(n_cols, tn)
    assert col0 % tn == 0
    nj, j0 = n_cols // tn, col0 // tn
    n_tiles = (t // tm) * nj
    hr = HALO_ROWS if halo else 0

    def cur(step):
        tile = jnp.minimum(step, n_tiles - 1)
        return tile // nj, tile % nj

    def out_tile(step):
        tile = jnp.maximum(step - 1, 0) if lag else step
        return tile // nj, tile % nj

    on_cur = lambda f: (lambda step: f(*cur(step)))
    on_out = lambda spec: pl.BlockSpec(spec.block_shape, lambda step, f=spec.index_map: f(*out_tile(step)))
    rows_mode = pl.Buffered(1) if (halo or tn >= 1024) else None
    in_specs = [pl.BlockSpec((tm, k), on_cur(lambda i, j: (i, 0)), pipeline_mode=rows_mode),
                pl.BlockSpec((tm, LANES), on_cur(lambda i, j: (i, 0)))]
    args = [xb, ss]
    scratch = [pltpu.VMEM((tm + hr, 1), F32)]
    if halo:
        assert seq_len % tm == 0 and tm % HALO_ROWS == 0
        per = tm // HALO_ROWS
        prev = on_cur(lambda i, j: (jnp.maximum(i * per - 1, 0), 0))
        in_specs += [pl.BlockSpec((HALO_ROWS, k), prev), pl.BlockSpec((HALO_ROWS, LANES), prev)]
        args += [xb, ss]
        scratch.append(pltpu.VMEM((tm + HALO_ROWS, k), BF16))
    if lag:
        scratch += [pltpu.VMEM((tm + hr, tn), F32)] * (2 * n_streams)
    for s in range(n_streams):
        in_specs.append(pl.BlockSpec((pl.Squeezed(), k, tn),
                                     on_cur(lambda i, j, s=s: (layer, 0, j0 + j + s * nj))))
        args.append(w)
    for arr, spec in extras:
        in_specs.append(on_out(spec))
        args.append(arr)
    body = functools.partial(_proj_kernel, n_streams=n_streams, n_extra=len(extras), n_out=len(outs),
                             epilogue=epilogue, halo=halo, lag=lag, seq_len=seq_len, k_dim=k, nj=nj,
                             n_tiles=n_tiles)
    return pl.pallas_call(
        body,
        grid=(n_tiles + int(lag),),
        in_specs=in_specs,
        out_specs=[on_out(spec) for _, spec in outs],
        out_shape=[shape for shape, _ in outs],
        scratch_shapes=scratch,
        compiler_params=_params("arbitrary"),
        name=name,
    )(*args)


def _row_col_tile(t, n, dtype, tn):
    tm, tn = _tile(t, ROW_TILE), _tile(n, tn)
    return jax.ShapeDtypeStruct((t, n), dtype), pl.BlockSpec((tm, tn), lambda i, j: (i, j))


def _col_vec(v, tn):
    return v, pl.BlockSpec((v.shape[0], tn), lambda i, j: (0, j))


def _conv_row_chunks(exts, w_ref, cols, row_chunk=CONV_ROW_CHUNK):
    width = w_ref.shape[0]
    tm = exts[0].shape[0] - HALO_ROWS
    chunk = _tile(tm, row_chunk)
    for r0 in range(0, tm, chunk):
        acc = None
        for tap in range(width):
            start = HALO_ROWS - (width - 1) + tap + r0
            term = functools.reduce(jnp.multiply, [e[start:start + chunk, :] for e in exts])
            term = term * w_ref[tap:tap + 1, cols]
            acc = term if acc is None else acc + term
        yield slice(r0, r0 + chunk), acc


def _heads(cols):
    return [(c // HEAD_DIM, slice(c, c + HEAD_DIM)) for c in range(cols.start, cols.stop, HEAD_DIM)]


def _swiglu_epilogue(accs, extra_refs, out_refs, cols):
    gate, up = accs
    out_refs[0][:, cols] = (_silu(gate) * up).astype(out_refs[0].dtype)


def _plain_epilogue(accs, extra_refs, out_refs, cols):
    out_refs[0][:, cols] = accs[0].astype(out_refs[0].dtype)


def _gdn_gates_kernel(ab_ref, alog_ref, dtb_ref, gc_ref, beta_ref, *, n_heads):
    c = GDN_CHUNK
    xs = ab_ref[...] + dtb_ref[...]
    softplus = jnp.maximum(xs, 0.0) + jnp.log(1.0 + jnp.exp(-jnp.abs(xs)))
    g = -jnp.exp(alog_ref[...]) * softplus
    row = lax.broadcasted_iota(jnp.int32, (c, c), 0)
    col = lax.broadcasted_iota(jnp.int32, (c, c), 1)
    tri = (row >= col).astype(BF16)
    for r in range(0, g.shape[0], c):
        hi, mid, lo = _split3(g[r:r + c, :])
        gc_ref[r:r + c, :] = (jnp.dot(tri, hi, preferred_element_type=F32)
                              + jnp.dot(tri, mid, preferred_element_type=F32)
                              + jnp.dot(tri, lo, preferred_element_type=F32))
    beta_ref[...] = jax.nn.sigmoid(pltpu.roll(ab_ref[...], LANES - n_heads, axis=1))


def gdn_gates(ab, a_log, dt_bias, *, tb):
    t = ab.shape[0]
    h = a_log.shape[0]
    assert 2 * h <= LANES
    tb = _tile(t, tb)
    pad = lambda v: jnp.pad(v.astype(F32), (0, LANES - h)).reshape(1, LANES)
    row_blk = pl.BlockSpec((tb, LANES), lambda i: (i, 0))
    vec = pl.BlockSpec((1, LANES), lambda i: (0, 0))
    return pl.pallas_call(
        functools.partial(_gdn_gates_kernel, n_heads=h),
        grid=(t // tb,),
        in_specs=[row_blk, vec, vec],
        out_specs=[row_blk, row_blk],
        out_shape=[jax.ShapeDtypeStruct((t, LANES), F32)] * 2,
        compiler_params=_params("parallel"),
        name="gdn_gates",
    )(ab, pad(a_log), pad(dt_bias))


def _shift_matrices(ts):
    n = GDN_CONV_TAPS - 1
    r = lax.broadcasted_iota(jnp.int32, (n * ts, ts), 0)
    c = lax.broadcasted_iota(jnp.int32, (n * ts, ts), 1)
    inner = (c + r // ts + 1 == r % ts).astype(BF16)
    r = lax.broadcasted_iota(jnp.int32, (n * HALO_ROWS, HALO_ROWS), 0)
    c = lax.broadcasted_iota(jnp.int32, (n * HALO_ROWS, HALO_ROWS), 1)
    edge = (c == HALO_ROWS - (r // HALO_ROWS + 1) + r % HALO_ROWS).astype(BF16)
    return inner, edge


def _causal_conv(x_ref, halo_ref, inner_ref, edge_ref, w_ref):
    x = x_ref[0]
    ts = x.shape[0]
    width = w_ref.shape[0]
    shifted = jnp.dot(inner_ref[...], x, preferred_element_type=F32)
    carried = jnp.dot(edge_ref[...], halo_ref[...], preferred_element_type=F32)
    acc = x.astype(F32) * w_ref[width - 1:width, :]
    for d in range(1, width):
        part = shifted[(d - 1) * ts:d * ts]
        top = part[0:HALO_ROWS] + carried[(d - 1) * HALO_ROWS:d * HALO_ROWS]
        part = jnp.concatenate([top, part[HALO_ROWS:]], axis=0)
        acc = acc + part * w_ref[width - 1 - d:width - d, :]
    halo_ref[...] = x[ts - HALO_ROWS:]
    return acc


def _gdn_chunk_kernel(q_ref, k_ref, v_ref, z_ref, wq_ref, wk_ref, wv_ref, gcc_ref, bc_ref, gcr_ref,
                      onorm_ref, o_ref, hq_ref, hk_ref, hv_ref, inner_ref, edge_ref, state_ref, *,
                      heads_per_step):
    c = GDN_CHUNK
    dk = HEAD_DIM
    seq_step = pl.program_id(2)
    ts = q_ref.shape[1]

    @pl.when(seq_step == 0)
    def _():
        state_ref[...] = jnp.zeros(state_ref.shape, F32)
        for halo_ref in (hq_ref, hk_ref, hv_ref):
            halo_ref[...] = jnp.zeros(halo_ref.shape, BF16)
        inner_ref[...], edge_ref[...] = _shift_matrices(ts)

    yq = _silu(_causal_conv(q_ref, hq_ref, inner_ref, edge_ref, wq_ref))
    yk = _silu(_causal_conv(k_ref, hk_ref, inner_ref, edge_ref, wk_ref))
    yv = _silu(_causal_conv(v_ref, hv_ref, inner_ref, edge_ref, wv_ref))

    row = lax.broadcasted_iota(jnp.int32, (c, c), 0)
    col = lax.broadcasted_iota(jnp.int32, (c, c), 1)
    incl = row >= col
    strict = row > col
    eye = (row == col).astype(F32)
    lane = lax.broadcasted_iota(jnp.int32, (ts, LANES), 1)
    hb, nc = heads_per_step, ts // c

    per_head = []
    for i in range(hb):
        head = pl.program_id(1) * hb + i
        hl = slice(i * dk, (i + 1) * dk)
        qh, kh, vh = yq[:, hl], yk[:, hl], yv[:, hl]
        qh = qh * lax.rsqrt(jnp.sum(qh * qh, axis=-1, keepdims=True) + EPS) * (dk ** -0.5)
        kh = kh * lax.rsqrt(jnp.sum(kh * kh, axis=-1, keepdims=True) + EPS)
        pick = lane == head
        gcol = jnp.sum(jnp.where(pick, gcc_ref[0], 0.0), axis=-1, keepdims=True)
        bcol = jnp.sum(jnp.where(pick, bc_ref[0], 0.0), axis=-1, keepdims=True)
        grow = gcr_ref[0, pl.ds(head, 1), :]
        per_head.append((qh, kh, vh, gcol, bcol, grow))
    items = [(ci, i) for ci in range(nc) for i in range(hb)]
    qs, ks, vs, gcols, bcols, grows = [], [], [], [], [], []
    for ci, i in items:
        qh, kh, vh, gcol, bcol, grow = per_head[i]
        sl = slice(ci * c, (ci + 1) * c)
        qs.append(qh[sl])
        ks.append(kh[sl])
        vs.append(vh[sl])
        gcols.append(gcol[sl])
        bcols.append(bcol[sl])
        grows.append(grow[:, sl])
    n = len(items)
    rng = range(n)
    gammas = [jnp.where(incl, jnp.exp(jnp.where(incl, gcols[t] - grows[t], 0.0)), 0.0) for t in rng]
    kbs = [ks[t] * bcols[t] for t in rng]
    n_pows = [jnp.where(strict, -(_dot_nt(kbs[t], ks[t]) * gammas[t]), 0.0) for t in rng]
    t_invs = [eye + n_pows[t] for t in rng]
    for _ in range(5):
        n_pows = [_dot(n_pows[t], n_pows[t]) for t in rng]
        t_invs = [t_invs[t] + _dot(t_invs[t], n_pows[t]) for t in rng]
    e_gs = [jnp.exp(gcols[t]) for t in rng]
    wus = [_dot(t_invs[t], jnp.concatenate([kbs[t] * e_gs[t], vs[t] * bcols[t]], axis=1)) for t in rng]
    qks = [jnp.where(incl, _dot_nt(qs[t], ks[t]) * gammas[t], 0.0) for t in rng]
    g_lasts = [grows[t][:, c - 1:c] for t in rng]
    q_decs = [qs[t] * e_gs[t] for t in rng]
    k_decs = [ks[t] * jnp.exp(g_lasts[t] - gcols[t]) for t in rng]
    decays = [jnp.exp(g_lasts[t]) for t in rng]

    states = [state_ref[i] for i in range(hb)]
    outs = [None] * n
    for ci in range(nc):
        idx = [ci * hb + i for i in range(hb)]
        v_news = [wus[t][:, dk:] - _dot(wus[t][:, :dk], states[i]) for i, t in enumerate(idx)]
        o_state = [_dot(q_decs[t], states[i]) for i, t in enumerate(idx)]
        for i, t in enumerate(idx):
            outs[t] = o_state[i] + _dot(qks[t], v_news[i])
        states = [states[i] * decays[t] + _dot_tn(k_decs[t], v_news[i]) for i, t in enumerate(idx)]
    for i in range(hb):
        state_ref[i] = states[i]

    for t, (ci, i) in enumerate(items):
        o = outs[t]
        o = o * lax.rsqrt(jnp.mean(o * o, axis=-1, keepdims=True) + EPS) * onorm_ref[...]
        z = z_ref[0, ci * c:(ci + 1) * c, i * dk:(i + 1) * dk].astype(F32)
        o_ref[0, ci * c:(ci + 1) * c, i * dk:(i + 1) * dk] = (o * _silu(z)).astype(o_ref.dtype)


def gdn_chunk(proj, conv_w, gcc, betac, gcr, o_norm, *, n_heads, ts, heads_per_step):
    b, s, _ = proj.shape
    dk = HEAD_DIM
    ts = _tile(s, ts)
    hb = _tile(n_heads, heads_per_step)
    groups = n_heads // hb
    width = conv_w.shape[0]
    assert ts % GDN_CHUNK == 0 and width == GDN_CONV_TAPS and width - 1 <= HALO_ROWS <= ts
    tok = lambda part: pl.BlockSpec((1, ts, hb * dk), lambda bi, hi, si, p=part: (bi, si, hi + p * groups))
    cw = lambda part: pl.BlockSpec((width, hb * dk), lambda bi, hi, si, p=part: (0, hi + p * groups))
    gate_col = pl.BlockSpec((1, ts, LANES), lambda bi, hi, si: (bi, si, 0))
    return pl.pallas_call(
        functools.partial(_gdn_chunk_kernel, heads_per_step=hb),
        grid=(b, groups, s // ts),
        in_specs=[tok(0), tok(1), tok(2), tok(3), cw(0), cw(1), cw(2), gate_col, gate_col,
                  pl.BlockSpec((1, n_heads, ts), lambda bi, hi, si: (bi, 0, si)),
                  pl.BlockSpec((1, dk), lambda bi, hi, si: (0, 0))],
        out_specs=pl.BlockSpec((1, ts, hb * dk), lambda bi, hi, si: (bi, si, hi)),
        out_shape=jax.ShapeDtypeStruct((b, s, n_heads * dk), BF16),
        scratch_shapes=[pltpu.VMEM((HALO_ROWS, hb * dk), BF16)] * 3
                       + [pltpu.VMEM(((GDN_CONV_TAPS - 1) * ts, ts), BF16),
                          pltpu.VMEM(((GDN_CONV_TAPS - 1) * HALO_ROWS, HALO_ROWS), BF16),
                          pltpu.VMEM((hb, dk, dk), F32)],
        compiler_params=_params("parallel", "parallel", "arbitrary"),
        name="gdn_chunk",
    )(proj, proj, proj, proj, conv_w, conv_w, conv_w, gcc, betac, gcr,
      o_norm.reshape(1, dk).astype(F32))


def gdn_cast_weights(w_in, n_heads):
    wide = 4 * n_heads * HEAD_DIM
    return jnp.pad(w_in, ((0, 0), (0, 0), (0, wide + LANES - w_in.shape[2]))).astype(BF16)


def gated_deltanet(x, xb, ss, w_all, layer, conv_w, a_log, dt_bias, o_norm, w_out,
                   next_gain, batch):
    t, d = x.shape
    n_heads = a_log.shape[0]
    s = t // batch
    wide = 4 * n_heads * HEAD_DIM
    (proj,) = projection(xb, ss, w_all, layer, name="gdn_in_proj", col0=0, n_cols=wide, n_streams=1,
                         epilogue=_plain_epilogue, outs=[_row_col_tile(t, wide, BF16, 1024)], tn=1024)
    (ab,) = projection(xb, ss, w_all, layer, name="gdn_ab_proj", col0=wide, n_cols=LANES,
                       n_streams=1, epilogue=_plain_epilogue,
                       outs=[_row_col_tile(t, LANES, F32, LANES)], tn=LANES)
    gcc, betac = gdn_gates(ab, a_log, dt_bias, tb=512)
    gcc = gcc.reshape(batch, s, LANES)
    betac = betac.reshape(batch, s, LANES)
    gcr = jnp.swapaxes(gcc[:, :, :n_heads], 1, 2)
    o = gdn_chunk(proj.reshape(batch, s, wide), conv_w.astype(F32), gcc, betac, gcr, o_norm,
                  n_heads=n_heads, ts=256, heads_per_step=16)
    return matmul_residual(o.reshape(t, n_heads * HEAD_DIM), w_out, layer, x, next_gain,
                           name="gdn_out_proj", scale=1.0)


def _shortconv_epilogue(accs, extra_refs, out_refs, cols):
    gb, gc, xv = accs
    for rows, conv in _conv_row_chunks([gc, xv], extra_refs[0], cols):
        gate = gb[HALO_ROWS + rows.start:HALO_ROWS + rows.stop, :]
        out_refs[0][rows, cols] = (gate * conv).astype(out_refs[0].dtype)


def short_conv(x, xb, ss, w_in, layer, conv_w, w_out, next_gain, batch):
    t, d = x.shape
    tn = 256
    (y,) = projection(xb, ss, w_in, layer, name="sc_in_proj", col0=0, n_cols=d, n_streams=3,
                      epilogue=_shortconv_epilogue, outs=[_row_col_tile(t, d, BF16, tn)], tn=tn,
                      extras=[_col_vec(conv_w.astype(F32), tn)], halo=True, seq_len=t // batch)
    return matmul_residual(y, w_out, layer, x, next_gain, name="sc_out_proj", scale=1.0)


def _rope_table_kernel(pos_ref, freq_ref, cos_ref, sin_ref):
    ang = pos_ref[0].astype(F32) * freq_ref[...]
    lane = lax.broadcasted_iota(jnp.int32, ang.shape, 1)
    sign = jnp.where(lane < HEAD_DIM // 2, -1.0, 1.0)
    cos_ref[0] = jnp.cos(ang)
    sin_ref[0] = jnp.sin(ang) * sign


def rope_tables(positions, *, ts):
    b, s = positions.shape
    ts = _tile(s, ts)
    half = HEAD_DIM // 2
    inv_freq = ROPE_THETA ** (-jnp.arange(half, dtype=F32) / half)
    freq = jnp.concatenate([inv_freq, inv_freq]).reshape(1, HEAD_DIM)
    blk = pl.BlockSpec((1, ts, HEAD_DIM), lambda bi, si: (bi, si, 0))
    return pl.pallas_call(
        _rope_table_kernel,
        grid=(b, s // ts),
        in_specs=[pl.BlockSpec((1, ts, 1), lambda bi, si: (bi, si, 0)),
                  pl.BlockSpec((1, HEAD_DIM), lambda bi, si: (0, 0))],
        out_specs=[blk, blk],
        out_shape=[jax.ShapeDtypeStruct((b, s, HEAD_DIM), F32)] * 2,
        compiler_params=_params("parallel", "parallel"),
        name="rope_tables",
    )(positions.reshape(b, s, 1), freq)


def _norm_rope(seg, cos, sin, gain):
    seg = seg * lax.rsqrt(jnp.mean(seg * seg, axis=-1, keepdims=True) + EPS) * gain
    return seg * cos + pltpu.roll(seg, HEAD_DIM // 2, axis=1) * sin


def _moba_q_epilogue(accs, extra_refs, out_refs, cols):
    cos_ref, sin_ref, gain_ref = extra_refs
    cos, sin, gain = cos_ref[...], sin_ref[...], gain_ref[...]
    acc = accs[0]
    for _, hl in _heads(cols):
        seg = acc[:, hl.start - cols.start:hl.stop - cols.start]
        qh = _norm_rope(seg, cos, sin, gain) * (HEAD_DIM ** -0.5 * LOG2_E)
        out_refs[0][:, hl] = qh.astype(out_refs[0].dtype)


def _moba_k_epilogue(accs, extra_refs, out_refs, cols):
    cos_ref, sin_ref, gain_ref = extra_refs
    cos, sin, gain = cos_ref[...], sin_ref[...], gain_ref[...]
    ko_ref, km_ref = out_refs
    acc = accs[0]
    bs = MOBA_BLOCK
    for h, hl in _heads(cols):
        kh = _norm_rope(acc[:, hl.start - cols.start:hl.stop - cols.start], cos, sin, gain)
        for g in range(acc.shape[0] // bs):
            blk = kh[g * bs:(g + 1) * bs]
            ko_ref[0, h, g] = blk.astype(ko_ref.dtype)
            km_ref[g, :, hl] = jnp.mean(blk, axis=0, keepdims=True)


def _moba_v_epilogue(accs, extra_refs, out_refs, cols):
    vto_ref = out_refs[0]
    acc = accs[0]
    bs, dh = MOBA_BLOCK, HEAD_DIM
    sub = lax.broadcasted_iota(jnp.int32, (SUM_ROWS, bs), 0)
    ones_rows = jnp.where(sub == 0, 1.0, 0.0).astype(vto_ref.dtype)
    for h, hl in _heads(cols):
        for g in range(acc.shape[0] // bs):
            blk = acc[g * bs:(g + 1) * bs, hl.start - cols.start:hl.stop - cols.start]
            vto_ref[0, h, g, 0:dh, :] = blk.T.astype(vto_ref.dtype)
            vto_ref[0, h, g, dh:dh + SUM_ROWS, :] = ones_rows


def _moba_attn_kernel(q_ref, k_ref, vt_ref, km_ref, o_ref, bias_ref, *, group, slab, heads):
    bs, dh = MOBA_BLOCK, HEAD_DIM
    nb = km_ref.shape[2]
    qi = pl.program_id(2)
    hrange = range(heads)
    q = [q_ref[0, :, h * dh:(h + 1) * dh] for h in hrange]

    blk = lax.broadcasted_iota(jnp.int32, (nb, bs), 0)
    blk_f = blk.astype(F32)
    past = blk < qi
    for h in hrange:
        gate = functools.reduce(jnp.add, [_dot_nt(part, q[h]) for part in _split3(km_ref[0, h])])
        gate = jnp.where(past, gate, -jnp.inf)
        bias = jnp.full((nb, bs), -jnp.inf, F32)
        for _ in range(MOBA_TOPK):
            top = jnp.max(gate, axis=0, keepdims=True)
            first = jnp.min(jnp.where(gate == top, blk_f, float(nb)), axis=0, keepdims=True)
            hit = (blk_f == first) & (top > -jnp.inf)
            bias = jnp.where(hit, 0.0, bias)
            gate = jnp.where(hit, -jnp.inf, gate)
        bias_ref[h] = bias

    def slab_scores(h, j, n):
        return _dot_nt(k_ref[0, h, pl.ds(j, n)].reshape(n * bs, dh), q[h])

    def pieces(s_t):
        m_blk = jnp.max(s_t, axis=0, keepdims=True)
        return m_blk, jnp.exp2(s_t - m_blk).astype(BF16)

    def values(h, j, p):
        return jnp.dot(vt_ref[0, h, j], p, preferred_element_type=F32)[:dh + 8]

    def merge(m, acc, parts):
        m_new = functools.reduce(jnp.maximum, [m] + [mb for mb, _ in parts])
        acc = jnp.exp2(m - m_new) * acc
        for mb, ab in parts:
            acc = acc + jnp.exp2(mb - m_new) * ab
        return m_new, acc

    key_pos = lax.broadcasted_iota(jnp.int32, (bs, bs), 0)
    qry_pos = lax.broadcasted_iota(jnp.int32, (bs, bs), 1)
    own = [slab_scores(h, qi, 1) for h in hrange]
    own = [pieces(jnp.where(key_pos <= qry_pos, s_t, -jnp.inf)) for s_t in own]
    carry = tuple((own[h][0], values(h, qi, own[h][1])) for h in hrange)

    def blocks_from(j0, n, carry):
        slabs = [[slab_scores(h, j0 + u * slab, slab) for u in range(n // slab)] for h in hrange]
        mp = [[pieces(slabs[h][b // slab][(b % slab) * bs:(b % slab + 1) * bs]) for b in range(n)]
              for h in hrange]
        parts = [[(mp[h][b][0] + bias_ref[h, pl.ds(j0 + b, 1), :], values(h, j0 + b, mp[h][b][1]))
                  for b in range(n)] for h in hrange]
        return tuple(merge(*carry[h], parts[h]) for h in hrange)

    half = group // 2
    rest = qi % group
    short = (half >= slab) & (rest >= 1) & (rest <= half)
    n_full = qi // group + jnp.where((rest >= 1) & jnp.logical_not(short), 1, 0)
    carry = lax.fori_loop(0, n_full, lambda it, c: blocks_from(it * group, group, c), carry)
    if half >= slab:
        tail = (qi // group) * group
        carry = lax.fori_loop(0, jnp.where(short, 1, 0), lambda it, c: blocks_from(tail, half, c), carry)
    for h in hrange:
        acc = carry[h][1]
        o_ref[0, :, h * dh:(h + 1) * dh] = (acc[:dh] / acc[dh:dh + 1]).T.astype(o_ref.dtype)


def moba_attn(q, k, vt, kmean):
    b, s, d = q.shape
    _, n_heads, nb, bs, dh = k.shape
    group = _tile(nb, 8)
    slab = _tile(group, 2)
    heads = _tile(n_heads, 4)
    return pl.pallas_call(
        functools.partial(_moba_attn_kernel, group=group, slab=slab, heads=heads),
        grid=(b, n_heads // heads, nb),
        in_specs=[pl.BlockSpec((1, bs, heads * dh), lambda bi, hi, qi: (bi, qi, hi)),
                  pl.BlockSpec((1, heads, nb, bs, dh), lambda bi, hi, qi: (bi, hi, 0, 0, 0)),
                  pl.BlockSpec((1, heads, nb, dh + SUM_ROWS, bs), lambda bi, hi, qi: (bi, hi, 0, 0, 0)),
                  pl.BlockSpec((1, heads, nb, dh), lambda bi, hi, qi: (bi, hi, 0, 0))],
        out_specs=pl.BlockSpec((1, bs, heads * dh), lambda bi, hi, qi: (bi, qi, hi)),
        out_shape=jax.ShapeDtypeStruct((b, s, d), BF16),
        scratch_shapes=[pltpu.VMEM((heads, nb, bs), F32)],
        compiler_params=_params("parallel", "parallel", "arbitrary"),
        name="moba_attn",
    )(q, k, vt, kmean)


def moba_attention(x, xb, ss, positions, w_in, layer, q_norm, k_norm, w_out, next_gain):
    t, d = x.shape
    batch, s = positions.shape
    n_heads = d // HEAD_DIM
    dh, bs = HEAD_DIM, MOBA_BLOCK
    assert s % bs == 0
    nb = s // bs
    tm = _tile(t, ROW_TILE)
    assert tm % bs == 0 and s % tm == 0
    tn = _tile(d, 512)
    hpt, bpt, tiles_per_seq = tn // dh, tm // bs, s // tm
    cos, sin = rope_tables(positions, ts=512)
    rows = pl.BlockSpec((tm, dh), lambda i, j: (i, 0))
    vec = pl.BlockSpec((1, dh), lambda i, j: (0, 0))
    rope_extras = lambda gain: [(cos.reshape(t, dh), rows), (sin.reshape(t, dh), rows),
                                (gain.reshape(1, dh).astype(F32), vec)]
    blocked = lambda i, j: (i // tiles_per_seq, j, i % tiles_per_seq, 0, 0)
    (q,) = projection(xb, ss, w_in, layer, name="moba_q_proj", col0=0, n_cols=d, n_streams=1,
                      epilogue=_moba_q_epilogue, outs=[_row_col_tile(t, d, BF16, tn)], tn=tn,
                      extras=rope_extras(q_norm), lag=True)
    k, kmean = projection(
        xb, ss, w_in, layer, name="moba_k_proj", col0=d, n_cols=d, n_streams=1,
        epilogue=_moba_k_epilogue, tn=tn, extras=rope_extras(k_norm), lag=True,
        outs=[(jax.ShapeDtypeStruct((batch, n_heads, nb, bs, dh), BF16),
               pl.BlockSpec((1, hpt, bpt, bs, dh), blocked)),
              (jax.ShapeDtypeStruct((t // bs, 1, d), F32),
               pl.BlockSpec((bpt, 1, tn), lambda i, j: (i, 0, j)))])
    (vt,) = projection(
        xb, ss, w_in, layer, name="moba_v_proj", col0=2 * d, n_cols=d, n_streams=1,
        epilogue=_moba_v_epilogue, tn=tn, lag=True,
        outs=[(jax.ShapeDtypeStruct((batch, n_heads, nb, dh + SUM_ROWS, bs), BF16),
               pl.BlockSpec((1, hpt, bpt, dh + SUM_ROWS, bs), blocked))])
    kmean = kmean.reshape(batch, nb, n_heads, dh).transpose(0, 2, 1, 3)
    o = moba_attn(q.reshape(batch, s, d), k, vt, kmean)
    return matmul_residual(o.reshape(t, d), w_out, layer, x, next_gain, name="moba_out_proj", scale=1.0)


def swiglu_half_step(x, xb, ss, w_in, w_out, layer, next_gain):
    t = x.shape[0]
    f = w_in.shape[2] // 2
    tn = 512
    (act,) = projection(xb, ss, w_in, layer, name="ffn_in", col0=0, n_cols=f, n_streams=2,
                        epilogue=_swiglu_epilogue, outs=[_row_col_tile(t, f, BF16, tn)], tn=tn)
    return matmul_residual(act, w_out, layer, x, next_gain, name="ffn_out", scale=0.5)


def kernel(x, positions, norm_ffn1, norm_mix, norm_ffn2, ffn1_w_in, ffn1_w_out, ffn2_w_in, ffn2_w_out, gdn_w_in, gdn_conv_w, gdn_a_log, gdn_dt_bias, gdn_out_norm, gdn_w_out, sc_w_in, sc_conv_w, sc_w_out, moba_w_in, moba_q_norm, moba_k_norm, moba_w_out):
    batch, s, d = x.shape
    depth = norm_ffn1.shape[0]
    ffn1_w_in, ffn1_w_out, ffn2_w_in, ffn2_w_out, gdn_w_out, sc_w_in, sc_w_out, moba_w_in, moba_w_out = (
        w.astype(BF16) for w in (ffn1_w_in, ffn1_w_out, ffn2_w_in, ffn2_w_out, gdn_w_out, sc_w_in,
                                 sc_w_out, moba_w_in, moba_w_out))
    gdn_w_all = gdn_cast_weights(gdn_w_in, gdn_a_log.shape[1])
    xt = x.reshape(batch * s, d)
    xb, ss = scale_rows(xt, norm_ffn1[0], tm=256)
    for i in range(depth):
        kind, j = i % 3, i // 3
        after_layer = norm_ffn1[i + 1] if i + 1 < depth else None
        xt, xb, ss = swiglu_half_step(xt, xb, ss, ffn1_w_in, ffn1_w_out, i, norm_mix[i])
        if kind == 0:
            xt, xb, ss = gated_deltanet(xt, xb, ss, gdn_w_all, j, gdn_conv_w[j], gdn_a_log[j],
                                        gdn_dt_bias[j], gdn_out_norm[j], gdn_w_out, norm_ffn2[i], batch)
        elif kind == 1:
            xt, xb, ss = short_conv(xt, xb, ss, sc_w_in, j, sc_conv_w[j], sc_w_out, norm_ffn2[i], batch)
        else:
            xt, xb, ss = moba_attention(xt, xb, ss, positions, moba_w_in, j, moba_q_norm[j],
                                        moba_k_norm[j], moba_w_out, norm_ffn2[i])
        xt, xb, ss = swiglu_half_step(xt, xb, ss, ffn2_w_in, ffn2_w_out, i, after_layer)
    return xt.reshape(batch, s, d)
```
